```python
import math
import jax, jax.numpy as jnp
from jax import lax

D_MODEL = 2048
BATCH = 8
SEQ = 4096
DEPTH = 4

GRID_W = 64
CTX_LEN = 256

N_SUB = 3
N_MOD = 3 * N_SUB
ALPHA = (2 * DEPTH) ** 0.25
BETA = (8 * DEPTH) ** -0.25
LN_EPS = 1e-6
RMS_EPS = 1e-6
D_FF = 5632
BRANCH_DIM = D_MODEL // 2
N_BRANCH = 3
SSD_DIM = BRANCH_DIM
SSD_HEAD_DIM = 64
SSD_HEADS = SSD_DIM // SSD_HEAD_DIM
SSD_GROUPS = 2
HEADS_PER_GROUP = SSD_HEADS // SSD_GROUPS
SSD_STATE = 128
SSD_CHUNK = 128
D_CONV = 5
CONV_PAD = D_CONV // 2
CONV_DIM = SSD_DIM + 2 * SSD_GROUPS * SSD_STATE
FOURIER_DIM = BRANCH_DIM
FOURIER_GROUPS = 4
FOURIER_GROUP_DIM = FOURIER_DIM // FOURIER_GROUPS
HEAD_DIM = 128
N_Q_HEADS = BRANCH_DIM // HEAD_DIM
N_KV_HEADS = 2
Q_PER_KV = N_Q_HEADS // N_KV_HEADS
ATTN_DIM = N_Q_HEADS * HEAD_DIM
KV_DIM = N_KV_HEADS * HEAD_DIM
AXIS_ROPE_DIM = HEAD_DIM // 2
ROPE_THETA = 10000.0
Q_BLOCK = 128
IN_WIDTHS = (SSD_DIM, CONV_DIM, SSD_HEADS, SSD_HEADS, FOURIER_DIM, ATTN_DIM, KV_DIM, KV_DIM, N_BRANCH * D_MODEL)
IN_SPLITS = tuple(sum(IN_WIDTHS[:i + 1]) for i in range(len(IN_WIDTHS) - 1))
D_IN = sum(IN_WIDTHS)

kernel_name = "hybrid_ssd_fourier_gqa_deepnorm_macaron"

F32 = jnp.float32


def layer_norm(x, g, b):
    xf = x.astype(F32)
    mu = jnp.mean(xf, -1, keepdims=True)
    var = jnp.mean(jnp.square(xf - mu), -1, keepdims=True)
    return ((xf - mu) * lax.rsqrt(var + LN_EPS) * g + b).astype(x.dtype)


def rms_norm(x, g):
    xf = x.astype(F32)
    return (xf * lax.rsqrt(jnp.mean(jnp.square(xf), -1, keepdims=True) + RMS_EPS) * g).astype(x.dtype)


def adaln(cvec, w, b):
    m = jax.nn.silu(cvec) @ w + b
    return m.reshape(cvec.shape[0], N_MOD, D_MODEL)


def modulate(s, mod, i):
    return s * (1 + mod[:, 3 * i + 1, None]) + mod[:, 3 * i, None]


def residual_norm(s, out, mod, i, g, b, weight):
    return layer_norm(ALPHA * s + weight * mod[:, 3 * i + 2, None] * out, g, b)


def swiglu(h, wgu, wd):
    gate, up = jnp.split(h @ wgu, 2, axis=-1)
    return (jax.nn.silu(gate) * up) @ wd


def ffn_sublayer(s, mod, i, wgu, wd, g, b):
    return residual_norm(s, swiglu(modulate(s, mod, i), wgu, wd), mod, i, g, b, 0.5)


def depthwise_conv(u, w, b):
    out = lax.conv_general_dilated(u, w[:, None, :].astype(u.dtype), window_strides=(1,),
                                   padding=((CONV_PAD, CONV_PAD),),
                                   dimension_numbers=("NWC", "WIO", "NWC"),
                                   feature_group_count=u.shape[-1])
    return out + b


def ssd_prepare(xbc, conv_w, conv_b):
    bsz, L, _ = xbc.shape
    xbc = jax.nn.silu(depthwise_conv(xbc, conv_w, conv_b))
    xs, bm, cm = jnp.split(xbc, (SSD_DIM, SSD_DIM + SSD_GROUPS * SSD_STATE), axis=-1)
    return (xs.reshape(bsz, L, SSD_HEADS, SSD_HEAD_DIM),
            bm.reshape(bsz, L, SSD_GROUPS, SSD_STATE),
            cm.reshape(bsz, L, SSD_GROUPS, SSD_STATE))


def segsum(a):
    t = a.shape[-1]
    cs = jnp.cumsum(a, -1)
    mask = jnp.tril(jnp.ones((t, t), dtype=bool))
    return jnp.where(mask, cs[..., :, None] - cs[..., None, :], -jnp.inf)


def ssd_scan(xh, dt, a, bm, cm, init):
    bsz, L = xh.shape[:2]
    nc = L // SSD_CHUNK
    xc = (xh * dt[..., None]).reshape(bsz, nc, SSD_CHUNK, SSD_GROUPS, HEADS_PER_GROUP, SSD_HEAD_DIM)
    da = (dt * a).reshape(bsz, nc, SSD_CHUNK, SSD_GROUPS, HEADS_PER_GROUP).transpose(0, 3, 4, 1, 2)
    bc = bm.reshape(bsz, nc, SSD_CHUNK, SSD_GROUPS, SSD_STATE)
    cc = cm.reshape(bsz, nc, SSD_CHUNK, SSD_GROUPS, SSD_STATE)
    cs = jnp.cumsum(da, -1)
    lmat = jnp.exp(segsum(da))
    cb = jnp.einsum("bclgn,bcsgn->bgcls", cc, bc)
    y_diag = jnp.einsum("bgecls,bcsgep->bclgep", cb[:, :, None] * lmat, xc)
    decay_states = jnp.exp(cs[..., -1:] - cs)
    states = jnp.einsum("bclgn,bgecl,bclgep->bcgepn", bc, decay_states, xc)
    states = jnp.concatenate([init[:, None], states], axis=1)
    chunk_decay = jnp.exp(segsum(jnp.pad(cs[..., -1], ((0, 0), (0, 0), (0, 0), (1, 0)))))
    states = jnp.einsum("bgezc,bcgepn->bzgepn", chunk_decay, states)
    y_off = jnp.einsum("bclgn,bcgepn,bgecl->bclgep", cc, states[:, :-1], jnp.exp(cs))
    y = (y_diag + y_off).reshape(bsz, L, SSD_HEADS, SSD_HEAD_DIM)
    return y, states[:, -1]


def ssd_direction(xh, dt_raw, bm, cm, dt_bias, a_log, init, reverse):
    if reverse:
        xh, dt_raw, bm, cm = (jnp.flip(t, 1) for t in (xh, dt_raw, bm, cm))
    dt = jax.nn.softplus(dt_raw.astype(F32) + dt_bias.astype(F32))
    a = -jnp.exp(a_log.astype(F32))
    y, final = ssd_scan(xh.astype(F32), dt, a, bm.astype(F32), cm.astype(F32), init)
    return (jnp.flip(y, 1) if reverse else y), final


def ssd_out(y, xh, z, d_skip, norm_g):
    bsz, L = z.shape[:2]
    y = (y + d_skip.astype(F32)[:, None] * xh.astype(F32)).reshape(bsz, L, SSD_DIM)
    y = (y * jax.nn.silu(z.astype(F32))).reshape(bsz, L, SSD_GROUPS, SSD_DIM // SSD_GROUPS)
    y = y * lax.rsqrt(jnp.mean(jnp.square(y), -1, keepdims=True) + RMS_EPS)
    return (y.reshape(bsz, L, SSD_DIM) * norm_g).astype(z.dtype)


def fourier_mix(u):
    bsz, L, _ = u.shape
    ug = u.astype(F32).reshape(bsz, L, FOURIER_GROUPS, FOURIER_GROUP_DIM)
    return jnp.fft.fftn(ug, axes=(1, 3), norm="ortho").real.reshape(bsz, L, FOURIER_DIM).astype(u.dtype)


def axial_rope(n_tokens):
    n_rows = n_tokens // GRID_W
    row = jnp.repeat(jnp.arange(n_rows), GRID_W).astype(F32)
    col = jnp.tile(jnp.arange(GRID_W), n_rows).astype(F32)
    inv_freq = ROPE_THETA ** (-jnp.arange(0, AXIS_ROPE_DIM, 2, dtype=F32) / AXIS_ROPE_DIM)
    ang_r = row[:, None] * inv_freq
    ang_c = col[:, None] * inv_freq
    return jnp.cos(ang_r), jnp.sin(ang_r), jnp.cos(ang_c), jnp.sin(ang_c)


def rotate(v, cos, sin):
    v1, v2 = jnp.split(v, 2, axis=-1)
    cos, sin = cos[:, None], sin[:, None]
    return jnp.concatenate([v1 * cos - v2 * sin, v1 * sin + v2 * cos], axis=-1)


def apply_axial_rope(x, rope):
    cos_r, sin_r, cos_c, sin_c = rope
    xr, xc = jnp.split(x.astype(F32), 2, axis=-1)
    return jnp.concatenate([rotate(xr, cos_r, sin_r), rotate(xc, cos_c, sin_c)], axis=-1).astype(x.dtype)


def sdpa(q, k, v):
    s = jnp.einsum("bqkgd,bskd->bkgqs", q, k).astype(F32) * (HEAD_DIM ** -0.5)
    p = jax.nn.softmax(s, axis=-1).astype(v.dtype)
    return jnp.einsum("bkgqs,bskd->bqkgd", p, v)


def blocked_attention(q, k, v):
    bsz, L = q.shape[:2]
    nb = L // Q_BLOCK
    qb = q.reshape(bsz, nb, Q_BLOCK, N_KV_HEADS, Q_PER_KV, HEAD_DIM).swapaxes(0, 1)
    o = lax.map(lambda blk: sdpa(blk, k, v), qb)
    return o.swapaxes(0, 1).reshape(bsz, L, ATTN_DIM)


def attn_heads(q, k, v, q_norm_g, k_norm_g, rope):
    bsz, L = q.shape[:2]
    q = rms_norm(q.reshape(bsz, L, N_Q_HEADS, HEAD_DIM), q_norm_g)
    k = rms_norm(k.reshape(bsz, L, N_KV_HEADS, HEAD_DIM), k_norm_g)
    if rope is not None:
        q = apply_axial_rope(q, rope)
        k = apply_axial_rope(k, rope)
    return (q.reshape(bsz, L, N_KV_HEADS, Q_PER_KV, HEAD_DIM), k,
            v.reshape(bsz, L, N_KV_HEADS, HEAD_DIM))


def merge_branches(y_ssd, y_four, y_attn, gates, w_branch, w_out):
    g_s, g_f, g_a = jnp.split(jax.nn.sigmoid(gates), N_BRANCH, axis=-1)
    m = g_s * (y_ssd @ w_branch[0]) + g_f * (y_four @ w_branch[1]) + g_a * (y_attn @ w_branch[2])
    return m @ w_out


def mixer_sublayer(x, s, mod_l, mod_c, rope, w_in, conv_w, conv_b, dt_bias, a_log, d_skip, ssd_norm_g,
                   q_norm_g, k_norm_g, w_branch, w_out, ln_g, ln_b, update_ctx):
    def project(stream, mod):
        return jnp.split(modulate(stream, mod, 1) @ w_in, IN_SPLITS, axis=-1)

    z_l, xbc_l, dtf_l, dtb_l, u_l, q_l, k_l, v_l, g_l = project(x, mod_l)
    z_c, xbc_c, dtf_c, dtb_c, u_c, q_c, k_c, v_c, g_c = project(s, mod_c)

    xh_c, b_c, c_c = ssd_prepare(xbc_c, conv_w, conv_b)
    xh_l, b_l, c_l = ssd_prepare(xbc_l, conv_w, conv_b)
    zero = jnp.zeros((s.shape[0], SSD_GROUPS, HEADS_PER_GROUP, SSD_HEAD_DIM, SSD_STATE), F32)
    yf_c, h_fwd = ssd_direction(xh_c, dtf_c, b_c, c_c, dt_bias[0], a_log[0], zero, False)
    yb_c, h_bwd = ssd_direction(xh_c, dtb_c, b_c, c_c, dt_bias[1], a_log[1], zero, True)
    yf_l, _ = ssd_direction(xh_l, dtf_l, b_l, c_l, dt_bias[0], a_log[0], h_fwd, False)
    yb_l, _ = ssd_direction(xh_l, dtb_l, b_l, c_l, dt_bias[1], a_log[1], h_bwd, True)
    ssd_l = ssd_out(yf_l + yb_l, xh_l, z_l, d_skip, ssd_norm_g)

    qh_c, kh_c, vh_c = attn_heads(q_c, k_c, v_c, q_norm_g, k_norm_g, None)
    qh_l, kh_l, vh_l = attn_heads(q_l, k_l, v_l, q_norm_g, k_norm_g, rope)
    attn_l = blocked_attention(qh_l, jnp.concatenate([kh_c, kh_l], axis=1),
                               jnp.concatenate([vh_c, vh_l], axis=1))

    four_l = fourier_mix(u_l)

    x = residual_norm(x, merge_branches(ssd_l, four_l, attn_l, g_l, w_branch, w_out), mod_l, 1, ln_g, ln_b, 1.0)
    if update_ctx:
        ssd_c = ssd_out(yf_c + yb_c, xh_c, z_c, d_skip, ssd_norm_g)
        attn_c = sdpa(qh_c, kh_c, vh_c).reshape(s.shape[0], s.shape[1], ATTN_DIM)
        four_c = fourier_mix(u_c)
        s = residual_norm(s, merge_branches(ssd_c, four_c, attn_c, g_c, w_branch, w_out), mod_c, 1, ln_g, ln_b, 1.0)
    return x, s


def setup_inputs(seed: int = 0) -> dict:
    key = jax.random.key(seed)
    ks = jax.random.split(key, 24)

    def normal(k, shape, scale):
        return jax.random.normal(k, shape, F32) * scale

    dt0 = jnp.exp(jax.random.uniform(ks[13], (DEPTH, 2, SSD_HEADS), F32, math.log(1e-3), math.log(1e-1)))
    return {
        "x": normal(ks[0], (BATCH, SEQ, D_MODEL), 1.0),
        "c": normal(ks[1], (BATCH, D_MODEL), 1.0),
        "ctx": normal(ks[2], (BATCH, CTX_LEN, D_MODEL), 1.0),
        "c_ctx": normal(ks[3], (D_MODEL,), 1.0),
        "w_ada": normal(ks[4], (DEPTH, D_MODEL, N_MOD * D_MODEL), 0.5 * D_MODEL ** -0.5),
        "b_ada": normal(ks[5], (DEPTH, N_MOD * D_MODEL), 0.02),
        "ln_g": 1.0 + normal(ks[6], (DEPTH, N_SUB, D_MODEL), 0.05),
        "ln_b": normal(ks[7], (DEPTH, N_SUB, D_MODEL), 0.02),
        "ffn_wgu": normal(ks[8], (DEPTH, 2, D_MODEL, 2 * D_FF), D_MODEL ** -0.5),
        "ffn_wd": normal(ks[9], (DEPTH, 2, D_FF, D_MODEL), BETA * D_FF ** -0.5),
        "w_in": normal(ks[10], (DEPTH, D_MODEL, D_IN), D_MODEL ** -0.5),
        "conv_w": normal(ks[11], (DEPTH, D_CONV, CONV_DIM), D_CONV ** -0.5),
        "conv_b": normal(ks[12], (DEPTH, CONV_DIM), 0.02),
        "dt_bias": dt0 + jnp.log(-jnp.expm1(-dt0)),
        "a_log": jnp.log(jax.random.uniform(ks[14], (DEPTH, 2, SSD_HEADS), F32, 1.0, 16.0)),
        "d_skip": 1.0 + normal(ks[15], (DEPTH, SSD_HEADS), 0.1),
        "ssd_norm_g": 1.0 + normal(ks[16], (DEPTH, SSD_DIM), 0.05),
        "q_norm_g": 1.0 + normal(ks[17], (DEPTH, HEAD_DIM), 0.05),
        "k_norm_g": 1.0 + normal(ks[18], (DEPTH, HEAD_DIM), 0.05),
        "w_branch": normal(ks[19], (DEPTH, N_BRANCH, BRANCH_DIM, D_MODEL), BRANCH_DIM ** -0.5),
        "w_out": normal(ks[20], (DEPTH, D_MODEL, D_MODEL), BETA * D_MODEL ** -0.5),
    }


def reference(x, c, ctx, c_ctx, w_ada, b_ada, ln_g, ln_b, ffn_wgu, ffn_wd, w_in, conv_w, conv_b,
              dt_bias, a_log, d_skip, ssd_norm_g, q_norm_g, k_norm_g, w_branch, w_out):
    rope = axial_rope(x.shape[1])
    s = ctx
    for layer in range(DEPTH):
        last = layer == DEPTH - 1
        mod_l = adaln(c, w_ada[layer], b_ada[layer])
        mod_c = adaln(c_ctx[None], w_ada[layer], b_ada[layer])
        g, b = ln_g[layer], ln_b[layer]
        x = ffn_sublayer(x, mod_l, 0, ffn_wgu[layer, 0], ffn_wd[layer, 0], g[0], b[0])
        s = ffn_sublayer(s, mod_c, 0, ffn_wgu[layer, 0], ffn_wd[layer, 0], g[0], b[0])
        x, s = mixer_sublayer(x, s, mod_l, mod_c, rope, w_in[layer], conv_w[layer], conv_b[layer],
                              dt_bias[layer], a_log[layer], d_skip[layer], ssd_norm_g[layer],
                              q_norm_g[layer], k_norm_g[layer], w_branch[layer], w_out[layer],
                              g[1], b[1], not last)
        x = ffn_sublayer(x, mod_l, 2, ffn_wgu[layer, 1], ffn_wd[layer, 1], g[2], b[2])
        if not last:
            s = ffn_sublayer(s, mod_c, 2, ffn_wgu[layer, 1], ffn_wd[layer, 1], g[2], b[2])
    return x
```

```python
import functools
import math

import jax
import jax.numpy as jnp
from jax import lax
from jax.experimental import pallas as pl
from jax.experimental.pallas import tpu as pltpu

F32 = jnp.float32
BF16 = jnp.bfloat16

D_MODEL = 2048
DEPTH = 4
GRID_W = 64
N_MOD = 9
ALPHA = (2 * DEPTH) ** 0.25
LN_EPS = 1e-6
RMS_EPS = 1e-6
D_FF = 5632
BRANCH_DIM = D_MODEL // 2
N_BRANCH = 3
SSD_DIM = BRANCH_DIM
SSD_HEAD_DIM = 64
SSD_HEADS = SSD_DIM // SSD_HEAD_DIM
SSD_GROUPS = 2
HEADS_PER_GROUP = SSD_HEADS // SSD_GROUPS
SSD_STATE = 128
SSD_CHUNK = 128
D_CONV = 5
CONV_DIM = SSD_DIM + 2 * SSD_GROUPS * SSD_STATE
FOURIER_DIM = BRANCH_DIM
FOURIER_GROUPS = 4
FOURIER_GROUP_DIM = FOURIER_DIM // FOURIER_GROUPS
HEAD_DIM = 128
N_Q_HEADS = BRANCH_DIM // HEAD_DIM
N_KV_HEADS = 2
Q_PER_KV = N_Q_HEADS // N_KV_HEADS
ATTN_DIM = N_Q_HEADS * HEAD_DIM
KV_DIM = N_KV_HEADS * HEAD_DIM
AXIS_ROPE_DIM = HEAD_DIM // 2
ROPE_THETA = 10000.0
IN_WIDTHS = (SSD_DIM, CONV_DIM, SSD_HEADS, SSD_HEADS, FOURIER_DIM, ATTN_DIM, KV_DIM, KV_DIM, N_BRANCH * D_MODEL)
IN_SPLITS = tuple(sum(IN_WIDTHS[:i + 1]) for i in range(len(IN_WIDTHS) - 1))

COL_Z = 0
COL_XBC = COL_Z + SSD_DIM
COL_U = COL_XBC + CONV_DIM
COL_Q = COL_U + FOURIER_DIM
COL_K = COL_Q + ATTN_DIM
COL_V = COL_K + KV_DIM
COL_G = COL_V + KV_DIM
PROJ_W = COL_G + N_BRANCH * D_MODEL
DT_W = 128
MOD_ROWS = 16

TM = 512
TF = 512
TN_PROJ = 1024
TN_MERGE = 512
TN_ADA = 1024
CONV_ROWS = 256
CONV_HALO = 16
CONV_COLS = 512
TQ = 256
TM_FOURIER = 256
VMEM_LIMIT = 56 * 1024 * 1024


def _cparams(sem):
    return pltpu.CompilerParams(dimension_semantics=sem, vmem_limit_bytes=VMEM_LIMIT)


def _silu(x):
    return x * jax.nn.sigmoid(x)


def _softplus(x):
    return jnp.maximum(x, 0.0) + jnp.log1p(jnp.exp(-jnp.abs(x)))


def _layer_norm(v, g, b):
    mu = jnp.mean(v, axis=-1, keepdims=True)
    d = v - mu
    var = jnp.mean(d * d, axis=-1, keepdims=True)
    return d * lax.rsqrt(var + LN_EPS) * g + b


def _dot(a, b):
    return jnp.dot(a, b, preferred_element_type=F32)


def _dot_nt(a, b):
    return lax.dot_general(a, b, (((1,), (1,)), ((), ())), preferred_element_type=F32)


def _dot_tn(a, b):
    return lax.dot_general(a, b, (((0,), (0,)), ((), ())), preferred_element_type=F32)


def _split2(a):
    hi = a.astype(BF16)
    lo = (a - hi.astype(F32)).astype(BF16)
    return hi, lo


def _split3(a):
    hi = a.astype(BF16)
    r = a - hi.astype(F32)
    mid = r.astype(BF16)
    lo = (r - mid.astype(F32)).astype(BF16)
    return hi, mid, lo


def _ada_kernel(c_ref, w_ref, b_ref, o_ref):
    h = _silu(c_ref[...]).astype(BF16)
    o_ref[...] = _dot(h, w_ref[...].astype(BF16)) + b_ref[...]


def _adaln_all(cvec, w_ada, b_ada):
    n = N_MOD * D_MODEL
    out = pl.pallas_call(
        _ada_kernel,
        grid=(DEPTH, n // TN_ADA),
        in_specs=[
            pl.BlockSpec((MOD_ROWS, D_MODEL), lambda l, j: (0, 0)),
            pl.BlockSpec((None, D_MODEL, TN_ADA), lambda l, j: (l, 0, j)),
            pl.BlockSpec((None, 1, TN_ADA), lambda l, j: (l, 0, j)),
        ],
        out_specs=pl.BlockSpec((None, MOD_ROWS, TN_ADA), lambda l, j: (l, 0, j)),
        out_shape=jax.ShapeDtypeStruct((DEPTH, MOD_ROWS, n), F32),
        compiler_params=_cparams(("parallel", "parallel")),
        name="adaln",
    )(cvec, w_ada, b_ada.reshape(DEPTH, 1, n))
    return out.reshape(DEPTH, MOD_ROWS, N_MOD, D_MODEL)


def _mod_spec(layer, tiles_per_batch, n_batch):
    return pl.BlockSpec((None, None, N_MOD, D_MODEL),
                        lambda i, j: (layer, jnp.minimum(i // tiles_per_batch, n_batch), 0, 0))


def _ln_spec(layer, sub):
    return pl.BlockSpec((None, None, 1, D_MODEL), lambda i, j: (layer, sub, 0, 0))


def _ffn_kernel(x_ref, mod_ref, wg_ref, wu_ref, wd_ref, g_ref, b_ref, o_ref, h_ref, *, sub, nj):
    j = pl.program_id(1)

    @pl.when(j == 0)
    def _():
        shift = mod_ref[3 * sub:3 * sub + 1, :]
        scale = mod_ref[3 * sub + 1:3 * sub + 2, :]
        h_ref[...] = (x_ref[...] * (1.0 + scale) + shift).astype(BF16)

    h = h_ref[...]
    gate = _dot(h, wg_ref[...])
    up = _dot(h, wu_ref[...])
    act = (_silu(gate) * up).astype(BF16)
    part = _dot(act, wd_ref[...])

    @pl.when(j == 0)
    def _():
        o_ref[...] = part

    @pl.when(j > 0)
    def _():
        o_ref[...] += part

    @pl.when(j == nj - 1)
    def _():
        gain = mod_ref[3 * sub + 2:3 * sub + 3, :]
        v = ALPHA * x_ref[...] + 0.5 * gain * o_ref[...]
        o_ref[...] = _layer_norm(v, g_ref[...], b_ref[...])


def _ffn(xs, rows, mods, wgu, wd, ln_g, ln_b, *, layer, which, sub, tiles_per_batch, n_batch):
    nj = D_FF // TF
    return pl.pallas_call(
        functools.partial(_ffn_kernel, sub=sub, nj=nj),
        grid=(rows // TM, nj),
        in_specs=[
            pl.BlockSpec((TM, D_MODEL), lambda i, j: (i, 0)),
            _mod_spec(layer, tiles_per_batch, n_batch),
            pl.BlockSpec((None, None, D_MODEL, TF), lambda i, j: (layer, which, 0, j)),
            pl.BlockSpec((None, None, D_MODEL, TF), lambda i, j: (layer, which, 0, nj + j)),
            pl.BlockSpec((None, None, TF, D_MODEL), lambda i, j: (layer, which, j, 0)),
            _ln_spec(layer, sub),
            _ln_spec(layer, sub),
        ],
        out_specs=pl.BlockSpec((TM, D_MODEL), lambda i, j: (i, 0)),
        out_shape=jax.ShapeDtypeStruct((rows, D_MODEL), F32),
        scratch_shapes=[pltpu.VMEM((TM, D_MODEL), BF16)],
        compiler_params=_cparams(("parallel", "arbitrary")),
        name="ffn",
    )(xs, mods, wgu, wgu, wd, ln_g, ln_b)


def _inproj_kernel(x_ref, mod_ref, w_ref, wdt_ref, o_ref, dt_ref, h_ref):
    j = pl.program_id(1)

    @pl.when(j == 0)
    def _():
        shift = mod_ref[3:4, :]
        scale = mod_ref[4:5, :]
        h = (x_ref[...] * (1.0 + scale) + shift).astype(BF16)
        h_ref[...] = h
        dt_ref[...] = _dot(h, wdt_ref[...])

    o_ref[...] = _dot(h_ref[...], w_ref[...]).astype(BF16)


def _inproj(xs, mods, w_main, w_dt, *, layer, tiles_per_batch, n_batch):
    rows = xs.shape[0]
    return pl.pallas_call(
        _inproj_kernel,
        grid=(rows // TM, PROJ_W // TN_PROJ),
        in_specs=[
            pl.BlockSpec((TM, D_MODEL), lambda i, j: (i, 0)),
            _mod_spec(layer, tiles_per_batch, n_batch),
            pl.BlockSpec((None, D_MODEL, TN_PROJ), lambda i, j: (layer, 0, j)),
            pl.BlockSpec((None, D_MODEL, DT_W), lambda i, j: (layer, 0, 0)),
        ],
        out_specs=[
            pl.BlockSpec((TM, TN_PROJ), lambda i, j: (i, j)),
            pl.BlockSpec((TM, DT_W), lambda i, j: (i, 0)),
        ],
        out_shape=[
            jax.ShapeDtypeStruct((rows, PROJ_W), BF16),
            jax.ShapeDtypeStruct((rows, DT_W), F32),
        ],
        scratch_shapes=[pltpu.VMEM((TM, D_MODEL), BF16)],
        compiler_params=_cparams(("parallel", "arbitrary")),
        name="inproj",
    )(xs, mods, w_main, w_dt)


def _conv_kernel(prev_ref, cur_ref, next_ref, w_ref, b_ref, o_ref, *, lat_tiles, tiles_per_seq):
    k = pl.program_id(0)
    in_ctx = k >= lat_tiles
    pos = k % tiles_per_seq
    first = jnp.logical_or(in_ctx, pos == 0)
    last = jnp.logical_or(in_ctx, pos == tiles_per_seq - 1)
    cur = cur_ref[...].astype(F32)
    n = cur.shape[0]
    prev = jnp.where(first, 0.0, prev_ref[...].astype(F32))
    nxt = jnp.where(last, 0.0, next_ref[...].astype(F32))
    row = lax.broadcasted_iota(jnp.int32, cur.shape, 0)
    w = w_ref[...]
    acc = cur * w[2:3, :] + b_ref[...]
    s = jnp.where(row == 0, prev[CONV_HALO - 1:CONV_HALO, :], pltpu.roll(cur, 1, 0))
    acc += s * w[1:2, :]
    s = jnp.where(row == 0, prev[CONV_HALO - 2:CONV_HALO - 1, :],
                  jnp.where(row == 1, prev[CONV_HALO - 1:CONV_HALO, :], pltpu.roll(cur, 2, 0)))
    acc += s * w[0:1, :]
    s = jnp.where(row == n - 1, nxt[0:1, :], pltpu.roll(cur, n - 1, 0))
    acc += s * w[3:4, :]
    s = jnp.where(row == n - 1, nxt[1:2, :],
                  jnp.where(row == n - 2, nxt[0:1, :], pltpu.roll(cur, n - 2, 0)))
    acc += s * w[4:5, :]
    o_ref[...] = _silu(acc).astype(BF16)


def _conv(proj, conv_w, conv_b, *, layer, lat_rows, seq_len):
    rows = proj.shape[0]
    halo_per_tile = CONV_ROWS // CONV_HALO
    n_halo = rows // CONV_HALO
    col0 = COL_XBC // CONV_COLS
    return pl.pallas_call(
        functools.partial(_conv_kernel, lat_tiles=lat_rows // CONV_ROWS, tiles_per_seq=seq_len // CONV_ROWS),
        grid=(rows // CONV_ROWS, CONV_DIM // CONV_COLS),
        in_specs=[
            pl.BlockSpec((CONV_HALO, CONV_COLS), lambda k, j: (jnp.maximum(k * halo_per_tile - 1, 0), col0 + j)),
            pl.BlockSpec((CONV_ROWS, CONV_COLS), lambda k, j: (k, col0 + j)),
            pl.BlockSpec((CONV_HALO, CONV_COLS),
                         lambda k, j: (jnp.minimum((k + 1) * halo_per_tile, n_halo - 1), col0 + j)),
            pl.BlockSpec((None, D_CONV, CONV_COLS), lambda k, j: (layer, 0, j)),
            pl.BlockSpec((None, 1, CONV_COLS), lambda k, j: (layer, 0, j)),
        ],
        out_specs=pl.BlockSpec((CONV_ROWS, CONV_COLS), lambda k, j: (k, j)),
        out_shape=jax.ShapeDtypeStruct((rows, CONV_DIM), BF16),
        compiler_params=_cparams(("parallel", "parallel")),
        name="conv",
    )(proj, proj, proj, conv_w, conv_b.reshape(DEPTH, 1, CONV_DIM))


def _qk_kernel(x_ref, g_ref, cos_ref, sin_ref, o_ref, *, lat_tiles):
    i = pl.program_id(0)
    h = pl.program_id(1)
    x = x_ref[...].astype(F32)
    xn = x * lax.rsqrt(jnp.mean(x * x, axis=-1, keepdims=True) + RMS_EPS) * g_ref[...]
    lane = lax.broadcasted_iota(jnp.int32, xn.shape, 1)
    half = AXIS_ROPE_DIM // 2
    partner = jnp.where((lane % AXIS_ROPE_DIM) < half,
                        pltpu.roll(xn, HEAD_DIM - half, 1), pltpu.roll(xn, half, 1))
    rot = xn * cos_ref[...] + partner * sin_ref[...]
    out = jnp.where(i < lat_tiles, rot, xn)
    out = out * jnp.where(h < N_Q_HEADS, HEAD_DIM ** -0.5, 1.0)
    o_ref[...] = out.astype(BF16)


def _qk_prep(proj, gains, cos_t, sin_t, *, lat_rows, seq_len):
    rows = proj.shape[0]
    n_heads = N_Q_HEADS + N_KV_HEADS
    col0 = COL_Q // HEAD_DIM
    tiles_per_seq = seq_len // TM
    return pl.pallas_call(
        functools.partial(_qk_kernel, lat_tiles=lat_rows // TM),
        grid=(rows // TM, n_heads),
        in_specs=[
            pl.BlockSpec((TM, HEAD_DIM), lambda i, h: (i, col0 + h)),
            pl.BlockSpec((None, 1, HEAD_DIM), lambda i, h: (h // N_Q_HEADS, 0, 0)),
            pl.BlockSpec((TM, HEAD_DIM), lambda i, h: (i % tiles_per_seq, 0)),
            pl.BlockSpec((TM, HEAD_DIM), lambda i, h: (i % tiles_per_seq, 0)),
        ],
        out_specs=pl.BlockSpec((TM, HEAD_DIM), lambda i, h: (i, h)),
        out_shape=jax.ShapeDtypeStruct((rows, n_heads * HEAD_DIM), BF16),
        compiler_params=_cparams(("parallel", "parallel")),
        name="qk_prep",
    )(proj, gains, cos_t, sin_t)


def _rope_tables(seq_len):
    t = jnp.arange(seq_len)
    row = (t // GRID_W).astype(F32)
    col = (t % GRID_W).astype(F32)
    inv_freq = ROPE_THETA ** (-jnp.arange(0, AXIS_ROPE_DIM, 2, dtype=F32) / AXIS_ROPE_DIM)
    ang_r = row[:, None] * inv_freq
    ang_c = col[:, None] * inv_freq
    cos_t = jnp.concatenate([jnp.cos(ang_r), jnp.cos(ang_r), jnp.cos(ang_c), jnp.cos(ang_c)], axis=-1)
    sin_t = jnp.concatenate([-jnp.sin(ang_r), jnp.sin(ang_r), -jnp.sin(ang_c), jnp.sin(ang_c)], axis=-1)
    return cos_t, sin_t


def _attn_kernel(q_ref, kl_ref, kc_ref, vl_ref, vc_ref, o_ref, *, nq):
    qi = pl.program_id(2)
    q = q_ref[...]
    sc = _dot_nt(q, kc_ref[...])

    @pl.when(qi < nq)
    def _():
        sl = _dot_nt(q, kl_ref[...])
        m = jnp.maximum(jnp.max(sl, axis=-1, keepdims=True), jnp.max(sc, axis=-1, keepdims=True))
        p_l = jnp.exp(sl - m)
        p_c = jnp.exp(sc - m)
        denom = jnp.sum(p_l, axis=-1, keepdims=True) + jnp.sum(p_c, axis=-1, keepdims=True)
        o = _dot(p_l.astype(BF16), vl_ref[...]) + _dot(p_c.astype(BF16), vc_ref[...])
        o_ref[...] = (o / denom).astype(BF16)

    @pl.when(qi == nq)
    def _():
        m = jnp.max(sc, axis=-1, keepdims=True)
        p_c = jnp.exp(sc - m)
        denom = jnp.sum(p_c, axis=-1, keepdims=True)
        o_ref[...] = (_dot(p_c.astype(BF16), vc_ref[...]) / denom).astype(BF16)


def _attention(qk, proj, *, n_batch, seq_len, ctx_len):
    rows = qk.shape[0]
    nq = seq_len // TQ
    ctx_blk0 = n_batch * seq_len // ctx_len
    kcol = N_Q_HEADS
    vcol = COL_V // HEAD_DIM

    def q_map(b, h, qi):
        return (jnp.where(qi < nq, b * nq + qi, n_batch * nq + b), h)

    return pl.pallas_call(
        functools.partial(_attn_kernel, nq=nq),
        grid=(n_batch, N_Q_HEADS, nq + 1),
        in_specs=[
            pl.BlockSpec((TQ, HEAD_DIM), q_map),
            pl.BlockSpec((seq_len, HEAD_DIM), lambda b, h, qi: (b, kcol + h // Q_PER_KV)),
            pl.BlockSpec((ctx_len, HEAD_DIM), lambda b, h, qi: (ctx_blk0 + b, kcol + h // Q_PER_KV)),
            pl.BlockSpec((seq_len, HEAD_DIM), lambda b, h, qi: (b, vcol + h // Q_PER_KV)),
            pl.BlockSpec((ctx_len, HEAD_DIM), lambda b, h, qi: (ctx_blk0 + b, vcol + h // Q_PER_KV)),
        ],
        out_specs=pl.BlockSpec((TQ, HEAD_DIM), q_map),
        out_shape=jax.ShapeDtypeStruct((rows, ATTN_DIM), BF16),
        compiler_params=_cparams(("parallel", "parallel", "arbitrary")),
        name="attention",
    )(qk, qk, qk, proj, proj)


def _ssd_kernel(x_ref, b_ref, c_ref, dt_ref, dtt_ref, bias_ref, biast_ref, alog_ref, alogt_ref,
                y_ref, st_ref):
    d = pl.program_id(1)
    c = pl.program_id(2)
    fwd = d == 0
    nh = SSD_HEADS
    gw = HEADS_PER_GROUP * SSD_HEAD_DIM
    n = SSD_CHUNK

    @pl.when(c == 0)
    def _():
        st_ref[...] = jnp.zeros_like(st_ref)

    row = lax.broadcasted_iota(jnp.int32, (n, n), 0)
    col = lax.broadcasted_iota(jnp.int32, (n, n), 1)
    sgn = jnp.where(fwd, 1, -1)
    mask = (row - col) * sgn >= 0
    tri = mask.astype(BF16)
    tri_t = ((col - row) * sgn >= 0).astype(BF16)

    dt_raw = dt_ref[...]
    dt_raw = jnp.where(fwd, dt_raw, pltpu.roll(dt_raw, DT_W - nh, 1))
    dtv = _softplus(dt_raw + bias_ref[...])
    da = dtv * (-jnp.exp(alog_ref[...]))
    da3 = _split3(da)
    cs = _dot(tri, da3[0]) + _dot(tri, da3[1]) + _dot(tri, da3[2])
    total = jnp.where(fwd, cs[n - 1:n, :], cs[0:1, :])

    start = pl.multiple_of(nh * d, nh)
    dtv_t = _softplus(dtt_ref[pl.ds(start, nh), :] + biast_ref[...])
    da_t = dtv_t * (-jnp.exp(alogt_ref[...]))
    da_t3 = _split3(da_t)
    cs_t = _dot(da_t3[0], tri_t) + _dot(da_t3[1], tri_t) + _dot(da_t3[2], tri_t)

    decay_end = jnp.exp(total - cs)
    ecs = jnp.exp(cs)
    lane = lax.broadcasted_iota(jnp.int32, (n, DT_W), 1)
    packed = jnp.where(lane < nh, dtv,
                       jnp.where(lane < 2 * nh, pltpu.roll(dtv * decay_end, nh, 1),
                                 jnp.where(lane < 3 * nh, pltpu.roll(ecs, 2 * nh, 1), 0.0)))
    er = lax.broadcasted_iota(jnp.int32, (DT_W, 3 * SSD_DIM), 0)
    ec = lax.broadcasted_iota(jnp.int32, (DT_W, 3 * SSD_DIM), 1)
    expand = (er == (ec // SSD_DIM) * nh + (ec % SSD_DIM) // SSD_HEAD_DIM).astype(BF16)
    p_hi, p_lo = _split2(packed)
    ex = _dot(p_hi, expand) + _dot(p_lo, expand)
    dt_x = ex[:, 0:SSD_DIM]
    dtdec_x = ex[:, SSD_DIM:2 * SSD_DIM]
    ecs_x = ex[:, 2 * SSD_DIM:3 * SSD_DIM]

    xf = x_ref[...].astype(F32)
    xdt = (xf * dt_x).astype(BF16)
    xdec = (xf * dtdec_x).astype(BF16)
    lane_h = lax.broadcasted_iota(jnp.int32, (n, 2 * SSD_HEAD_DIM), 1)

    for g in range(SSD_GROUPS):
        bg = b_ref[:, g * SSD_STATE:(g + 1) * SSD_STATE]
        cg = c_ref[:, g * SSD_STATE:(g + 1) * SSD_STATE]
        cb = _dot_nt(cg, bg)
        st = st_ref[g]
        y_off = _dot(cg, st.astype(BF16)) * ecs_x[:, g * gw:(g + 1) * gw]
        for j in range(HEADS_PER_GROUP // 2):
            ws = []
            for e in (2 * j, 2 * j + 1):
                hcol = g * HEADS_PER_GROUP + e
                diff = cs[:, hcol:hcol + 1] - cs_t[hcol:hcol + 1, :]
                lm = jnp.exp(jnp.where(mask, diff, -1e30))
                ws.append((cb * lm).astype(BF16))
            w_pair = jnp.concatenate(ws, axis=1)
            lo = g * gw + j * 2 * SSD_HEAD_DIM
            x2 = xdt[:, lo:lo + 2 * SSD_HEAD_DIM]
            rhs = jnp.concatenate([jnp.where(lane_h < SSD_HEAD_DIM, x2, jnp.zeros_like(x2)),
                                   jnp.where(lane_h >= SSD_HEAD_DIM, x2, jnp.zeros_like(x2))], axis=0)
            y_ref[:, lo:lo + 2 * SSD_HEAD_DIM] = (
                _dot(w_pair, rhs) + y_off[:, j * 2 * SSD_HEAD_DIM:(j + 1) * 2 * SSD_HEAD_DIM])
        new_states = _dot_tn(bg, xdec[:, g * gw:(g + 1) * gw])
        etot = jnp.where(fwd, ecs_x[n - 1:n, g * gw:(g + 1) * gw], ecs_x[0:1, g * gw:(g + 1) * gw])
        st_ref[g] = st * etot + new_states


def _ssd(xbc, dt, dt_t, bias, bias_t, alog, alog_t, *, layer, n_batch, seq_len, ctx_len):
    rows = xbc.shape[0]
    ncc = ctx_len // SSD_CHUNK
    ncl = seq_len // SSD_CHUNK
    lat_blk = seq_len // SSD_CHUNK
    ctx_blk0 = n_batch * seq_len // SSD_CHUNK

    def rb(b, d, c):
        cc = jnp.where(d == 0, c, ncc - 1 - c)
        lc = jnp.where(d == 0, c - ncc, ncl - 1 - (c - ncc))
        return jnp.where(c < ncc, ctx_blk0 + b * ncc + cc, b * lat_blk + lc)

    bcol = SSD_DIM // (SSD_GROUPS * SSD_STATE)
    return pl.pallas_call(
        _ssd_kernel,
        grid=(n_batch, 2, ncc + ncl),
        in_specs=[
            pl.BlockSpec((SSD_CHUNK, SSD_DIM), lambda b, d, c: (rb(b, d, c), 0)),
            pl.BlockSpec((SSD_CHUNK, SSD_GROUPS * SSD_STATE), lambda b, d, c: (rb(b, d, c), bcol)),
            pl.BlockSpec((SSD_CHUNK, SSD_GROUPS * SSD_STATE), lambda b, d, c: (rb(b, d, c), bcol + 1)),
            pl.BlockSpec((SSD_CHUNK, DT_W), lambda b, d, c: (rb(b, d, c), 0)),
            pl.BlockSpec((DT_W, SSD_CHUNK), lambda b, d, c: (0, rb(b, d, c))),
            pl.BlockSpec((None, None, 1, DT_W), lambda b, d, c: (layer, d, 0, 0)),
            pl.BlockSpec((None, None, SSD_HEADS, 1), lambda b, d, c: (layer, d, 0, 0)),
            pl.BlockSpec((None, None, 1, DT_W), lambda b, d, c: (layer, d, 0, 0)),
            pl.BlockSpec((None, None, SSD_HEADS, 1), lambda b, d, c: (layer, d, 0, 0)),
        ],
        out_specs=pl.BlockSpec((None, SSD_CHUNK, SSD_DIM), lambda b, d, c: (d, rb(b, d, c), 0)),
        out_shape=jax.ShapeDtypeStruct((2, rows, SSD_DIM), F32),
        scratch_shapes=[pltpu.VMEM((SSD_GROUPS, SSD_STATE, HEADS_PER_GROUP * SSD_HEAD_DIM), F32)],
        compiler_params=_cparams(("parallel", "parallel", "arbitrary")),
        name="ssd",
    )(xbc, xbc, xbc, dt, dt_t, bias, bias_t, alog, alog_t)


def _fourier_kernel(u0_ref, u1_ref, u2_ref, u3_ref, cs_ref, m_ref, *rest, seq_len, aliased):
    if aliased:
        _, o_ref, p_ref = rest
    else:
        o_ref, p_ref = rest
    mi = pl.program_id(1)
    gd = FOURIER_GROUP_DIM
    rows = min(seq_len, 512)

    @pl.when(mi == 0)
    def _():
        for g, u_ref in enumerate((u0_ref, u1_ref, u2_ref, u3_ref)):
            for r in range(seq_len // rows):
                ab = _dot(u_ref[r * rows:(r + 1) * rows, :], cs_ref[...])
                p_ref[r * rows:(r + 1) * rows, g * gd:(g + 1) * gd] = ab[:, :gd].astype(BF16)
                p_ref[seq_len + r * rows:seq_len + (r + 1) * rows, g * gd:(g + 1) * gd] = ab[:, gd:].astype(BF16)

    o_ref[...] = _dot(m_ref[...], p_ref[...]).astype(BF16)


def _fourier(proj, chan_mat, seq_mat, prev_out, *, n_batch, seq_len, row0, total_rows):
    tm = min(TM_FOURIER, seq_len)
    nm = seq_len // tm
    gd = FOURIER_GROUP_DIM
    col0 = COL_U // gd
    blk0 = row0 // seq_len
    oblk0 = row0 // tm
    aliased = prev_out is not None
    u_specs = [pl.BlockSpec((seq_len, gd), functools.partial(lambda b, mi, g: (blk0 + b, col0 + g), g=g))
               for g in range(FOURIER_GROUPS)]
    in_specs = u_specs + [
        pl.BlockSpec((gd, 2 * gd), lambda b, mi: (0, 0)),
        pl.BlockSpec((tm, 2 * seq_len), lambda b, mi: (mi, 0)),
    ]
    args = [proj, proj, proj, proj, chan_mat, seq_mat]
    if aliased:
        in_specs.append(pl.BlockSpec(memory_space=pl.ANY))
        args.append(prev_out)
    return pl.pallas_call(
        functools.partial(_fourier_kernel, seq_len=seq_len, aliased=aliased),
        grid=(n_batch, nm),
        in_specs=in_specs,
        out_specs=pl.BlockSpec((tm, FOURIER_DIM), lambda b, mi: (oblk0 + b * nm + mi, 0)),
        out_shape=jax.ShapeDtypeStruct((total_rows, FOURIER_DIM), BF16),
        scratch_shapes=[pltpu.VMEM((2 * seq_len, FOURIER_DIM), BF16)],
        input_output_aliases=({6: 0} if aliased else {}),
        compiler_params=_cparams(("parallel", "arbitrary")),
        name="fourier",
    )(*args)


def _dft_cos_sin(n):
    j = jnp.arange(n, dtype=jnp.int32)
    jk = (j[:, None] * j[None, :]) % n
    ang = jk.astype(F32) * (2.0 * math.pi / n)
    return jnp.cos(ang), jnp.sin(ang)


def _dft_tables(n):
    cos_m, sin_m = _dft_cos_sin(n)
    return (jnp.concatenate([cos_m, -sin_m], axis=1) * (n ** -0.5)).astype(BF16)


def _chan_table():
    cos_m, sin_m = _dft_cos_sin(FOURIER_GROUP_DIM)
    return (jnp.concatenate([cos_m, sin_m], axis=1) * (FOURIER_GROUP_DIM ** -0.5)).astype(BF16)


def _merge_kernel(x_ref, mod_ref, yf_ref, yb_ref, xh_ref, z_ref, four_ref, attn_ref, g0_ref, g1_ref, g2_ref,
                  wb0_ref, wb1_ref, wb2_ref, wo_ref, dsk_ref, ng_ref, lg_ref, lb_ref, o_ref, ys_ref, *, nj):
    j = pl.program_id(1)

    @pl.when(j == 0)
    def _():
        y = yf_ref[...] + yb_ref[...] + dsk_ref[...] * xh_ref[...].astype(F32)
        y = y * _silu(z_ref[...].astype(F32))
        gw = SSD_DIM // SSD_GROUPS
        for g in range(SSD_GROUPS):
            yg = y[:, g * gw:(g + 1) * gw]
            yg = yg * lax.rsqrt(jnp.mean(yg * yg, axis=-1, keepdims=True) + RMS_EPS)
            ys_ref[:, g * gw:(g + 1) * gw] = (yg * ng_ref[:, g * gw:(g + 1) * gw]).astype(BF16)

    m = (jax.nn.sigmoid(g0_ref[...].astype(F32)) * _dot(ys_ref[...], wb0_ref[...])
         + jax.nn.sigmoid(g1_ref[...].astype(F32)) * _dot(four_ref[...], wb1_ref[...])
         + jax.nn.sigmoid(g2_ref[...].astype(F32)) * _dot(attn_ref[...], wb2_ref[...]))
    part = _dot(m.astype(BF16), wo_ref[...])

    @pl.when(j == 0)
    def _():
        o_ref[...] = part

    @pl.when(j > 0)
    def _():
        o_ref[...] += part

    @pl.when(j == nj - 1)
    def _():
        v = ALPHA * x_ref[...] + mod_ref[5:6, :] * o_ref[...]
        o_ref[...] = _layer_norm(v, lg_ref[...], lb_ref[...])


def _merge(xs, rows, mods, y_ssd, xbc, proj, four, attn, w_branch, w_out, dsk, norm_g, ln_g, ln_b, *,
           layer, tiles_per_batch, n_batch):
    nj = D_MODEL // TN_MERGE
    gcol = COL_G // TN_MERGE
    gstep = D_MODEL // TN_MERGE

    def gate_spec(k):
        return pl.BlockSpec((TM, TN_MERGE), lambda i, j: (i, gcol + k * gstep + j))

    def wb_spec(k):
        return pl.BlockSpec((None, None, BRANCH_DIM, TN_MERGE), lambda i, j: (layer, k, 0, j))

    row_spec = pl.BlockSpec((TM, BRANCH_DIM), lambda i, j: (i, 0))
    vec_spec = pl.BlockSpec((None, 1, BRANCH_DIM), lambda i, j: (layer, 0, 0))
    return pl.pallas_call(
        functools.partial(_merge_kernel, nj=nj),
        grid=(rows // TM, nj),
        in_specs=[
            pl.BlockSpec((TM, D_MODEL), lambda i, j: (i, 0)),
            _mod_spec(layer, tiles_per_batch, n_batch),
            pl.BlockSpec((None, TM, SSD_DIM), lambda i, j: (0, i, 0)),
            pl.BlockSpec((None, TM, SSD_DIM), lambda i, j: (1, i, 0)),
            row_spec,
            row_spec,
            row_spec,
            row_spec,
            gate_spec(0), gate_spec(1), gate_spec(2),
            wb_spec(0), wb_spec(1), wb_spec(2),
            pl.BlockSpec((None, TN_MERGE, D_MODEL), lambda i, j: (layer, j, 0)),
            vec_spec, vec_spec,
            _ln_spec(layer, 1), _ln_spec(layer, 1),
        ],
        out_specs=pl.BlockSpec((TM, D_MODEL), lambda i, j: (i, 0)),
        out_shape=jax.ShapeDtypeStruct((rows, D_MODEL), F32),
        scratch_shapes=[pltpu.VMEM((TM, SSD_DIM), BF16)],
        compiler_params=_cparams(("parallel", "arbitrary")),
        name="merge",
    )(xs, mods, y_ssd, y_ssd, xbc, proj, four, attn, proj, proj, proj,
      w_branch, w_branch, w_branch, w_out, dsk, norm_g, ln_g, ln_b)


def kernel(x, c, ctx, c_ctx, w_ada, b_ada, ln_g, ln_b, ffn_wgu, ffn_wd, w_in, conv_w, conv_b, dt_bias, a_log,
           d_skip, ssd_norm_g, q_norm_g, k_norm_g, w_branch, w_out):
    n_batch, seq_len, _ = x.shape
    ctx_len = ctx.shape[1]
    lat_rows = n_batch * seq_len
    rows = lat_rows + n_batch * ctx_len
    assert ctx_len == TQ and seq_len % TM == 0 and (n_batch * ctx_len) % TM == 0 and n_batch < MOD_ROWS
    tiles_per_batch = seq_len // TM
    tile_kw = dict(tiles_per_batch=tiles_per_batch, n_batch=n_batch)

    wgu_bf = ffn_wgu.astype(BF16)
    wd_bf = ffn_wd.astype(BF16)
    wb_bf = w_branch.astype(BF16)
    wo_bf = w_out.astype(BF16)
    w_z, w_xbc, w_dtf, w_dtb, w_u, w_q, w_k, w_v, w_g = jnp.split(w_in, IN_SPLITS, axis=-1)
    w_main = jnp.concatenate([w_z, w_xbc, w_u, w_q, w_k, w_v, w_g], axis=-1).astype(BF16)
    w_dt = jnp.concatenate([w_dtf, w_dtb, jnp.zeros((DEPTH, D_MODEL, DT_W - 2 * SSD_HEADS), F32)],
                           axis=-1).astype(BF16)
    head_pad = ((0, 0), (0, 0), (0, DT_W - SSD_HEADS))
    bias_row = jnp.pad(dt_bias, head_pad).reshape(DEPTH, 2, 1, DT_W)
    alog_row = jnp.pad(a_log, head_pad).reshape(DEPTH, 2, 1, DT_W)
    bias_col = dt_bias.reshape(DEPTH, 2, SSD_HEADS, 1)
    alog_col = a_log.reshape(DEPTH, 2, SSD_HEADS, 1)
    dsk_all = jnp.repeat(d_skip, SSD_HEAD_DIM, axis=-1).reshape(DEPTH, 1, SSD_DIM)
    norm_g_all = ssd_norm_g.reshape(DEPTH, 1, SSD_DIM)
    qk_gains = jnp.stack([q_norm_g, k_norm_g], axis=1).reshape(DEPTH, 2, 1, HEAD_DIM)
    ln_g4 = ln_g.reshape(DEPTH, 3, 1, D_MODEL)
    ln_b4 = ln_b.reshape(DEPTH, 3, 1, D_MODEL)
    cos_t, sin_t = _rope_tables(seq_len)
    chan_mat = _chan_table()
    seq_mat_l = _dft_tables(seq_len)
    seq_mat_c = _dft_tables(ctx_len)

    cvec = jnp.concatenate([c, c_ctx[None], jnp.zeros((MOD_ROWS - n_batch - 1, D_MODEL), F32)], axis=0)
    mods = _adaln_all(cvec, w_ada, b_ada)

    xs = jnp.concatenate([x.reshape(lat_rows, D_MODEL), ctx.reshape(n_batch * ctx_len, D_MODEL)], axis=0)
    for layer in range(DEPTH):
        last = layer == DEPTH - 1
        xs = _ffn(xs, rows, mods, wgu_bf, wd_bf, ln_g4, ln_b4, layer=layer, which=0, sub=0, **tile_kw)

        proj, dt = _inproj(xs, mods, w_main, w_dt, layer=layer, **tile_kw)
        xbc = _conv(proj, conv_w, conv_b, layer=layer, lat_rows=lat_rows, seq_len=seq_len)
        y_ssd = _ssd(xbc, dt, dt.T, bias_row, bias_col, alog_row, alog_col,
                     layer=layer, n_batch=n_batch, seq_len=seq_len, ctx_len=ctx_len)
        qk = _qk_prep(proj, qk_gains[layer], cos_t, sin_t, lat_rows=lat_rows, seq_len=seq_len)
        attn = _attention(qk, proj, n_batch=n_batch, seq_len=seq_len, ctx_len=ctx_len)
        four = _fourier(proj, chan_mat, seq_mat_l, None, n_batch=n_batch, seq_len=seq_len, row0=0,
                        total_rows=rows)
        if not last:
            four = _fourier(proj, chan_mat, seq_mat_c, four, n_batch=n_batch, seq_len=ctx_len, row0=lat_rows,
                            total_rows=rows)
        out_rows = lat_rows if last else rows
        xs = _merge(xs, out_rows, mods, y_ssd, xbc, proj, four, attn, wb_bf, wo_bf, dsk_all, norm_g_all,
                    ln_g4, ln_b4, layer=layer, **tile_kw)
        xs = _ffn(xs, out_rows, mods, wgu_bf, wd_bf, ln_g4, ln_b4, layer=layer, which=1, sub=2, **tile_kw)
    return xs.reshape(n_batch, seq_len, D_MODEL)
```

```python
import functools
import math

import jax
import jax.numpy as jnp
from jax import lax
from jax.experimental import pallas as pl
from jax.experimental.pallas import tpu as pltpu

F32 = jnp.float32
BF16 = jnp.bfloat16

D_MODEL = 2048
DEPTH = 4
GRID_W = 64
N_MOD = 9
ALPHA = (2 * DEPTH) ** 0.25
LN_EPS = 1e-6
RMS_EPS = 1e-6
D_FF = 5632
BRANCH_DIM = D_MODEL // 2
N_BRANCH = 3
SSD_DIM = BRANCH_DIM
SSD_HEAD_DIM = 64
SSD_HEADS = SSD_DIM // SSD_HEAD_DIM
SSD_GROUPS = 2
HEADS_PER_GROUP = SSD_HEADS // SSD_GROUPS
SSD_STATE = 128
SSD_CHUNK = 128
D_CONV = 5
CONV_DIM = SSD_DIM + 2 * SSD_GROUPS * SSD_STATE
FOURIER_DIM = BRANCH_DIM
FOURIER_GROUPS = 4
FOURIER_GROUP_DIM = FOURIER_DIM // FOURIER_GROUPS
HEAD_DIM = 128
N_Q_HEADS = BRANCH_DIM // HEAD_DIM
N_KV_HEADS = 2
Q_PER_KV = N_Q_HEADS // N_KV_HEADS
ATTN_DIM = N_Q_HEADS * HEAD_DIM
KV_DIM = N_KV_HEADS * HEAD_DIM
AXIS_ROPE_DIM = HEAD_DIM // 2
ROPE_THETA = 10000.0
IN_WIDTHS = (SSD_DIM, CONV_DIM, SSD_HEADS, SSD_HEADS, FOURIER_DIM, ATTN_DIM, KV_DIM, KV_DIM, N_BRANCH * D_MODEL)
IN_SPLITS = tuple(sum(IN_WIDTHS[:i + 1]) for i in range(len(IN_WIDTHS) - 1))

COL_Z = 0
COL_U = COL_Z + SSD_DIM
COL_Q = COL_U + FOURIER_DIM
COL_XBC = COL_Q + ATTN_DIM
COL_K = COL_XBC + CONV_DIM
COL_V = COL_K + KV_DIM
COL_G = COL_V + KV_DIM
PROJ_W = COL_G + N_BRANCH * D_MODEL
DT_W = 128
MOD_ROWS = 16

TM = 512
TF = 512
TN_PROJ = 2816
FFN_SPLIT = 2
MERGE_SPLIT = 2
TN_MERGE = 512
TN_ADA = 1024
CONV_ROWS = 256
CONV_HALO = 16
TQ = 256
TM_FOURIER = 256
VMEM_LIMIT = 56 * 1024 * 1024


def _cparams(sem):
    return pltpu.CompilerParams(dimension_semantics=sem, vmem_limit_bytes=VMEM_LIMIT)


def _silu(x):
    return x * jax.nn.sigmoid(x)


def _softplus(x):
    return jnp.maximum(x, 0.0) + jnp.log1p(jnp.exp(-jnp.abs(x)))


def _layer_norm(v, g, b):
    mu = jnp.mean(v, axis=-1, keepdims=True)
    d = v - mu
    var = jnp.mean(d * d, axis=-1, keepdims=True)
    return d * lax.rsqrt(var + LN_EPS) * g + b


def _dot(a, b):
    return jnp.dot(a, b, preferred_element_type=F32)


def _dot_nt(a, b):
    return lax.dot_general(a, b, (((1,), (1,)), ((), ())), preferred_element_type=F32)


def _dot_tn(a, b):
    return lax.dot_general(a, b, (((0,), (0,)), ((), ())), preferred_element_type=F32)


def _split2(a):
    hi = a.astype(BF16)
    lo = (a - hi.astype(F32)).astype(BF16)
    return hi, lo


def _split3(a):
    hi = a.astype(BF16)
    r = a - hi.astype(F32)
    mid = r.astype(BF16)
    lo = (r - mid.astype(F32)).astype(BF16)
    return hi, mid, lo


def _ada_kernel(c_ref, w_ref, b_ref, o_ref):
    h = _silu(c_ref[...]).astype(BF16)
    o_ref[...] = _dot(h, w_ref[...].astype(BF16)) + b_ref[...]


def _adaln_all(cvec, w_ada, b_ada):
    n = N_MOD * D_MODEL
    out = pl.pallas_call(
        _ada_kernel,
        grid=(DEPTH, n // TN_ADA),
        in_specs=[
            pl.BlockSpec((MOD_ROWS, D_MODEL), lambda l, j: (0, 0)),
            pl.BlockSpec((None, D_MODEL, TN_ADA), lambda l, j: (l, 0, j)),
            pl.BlockSpec((None, 1, TN_ADA), lambda l, j: (l, 0, j)),
        ],
        out_specs=pl.BlockSpec((None, MOD_ROWS, TN_ADA), lambda l, j: (l, 0, j)),
        out_shape=jax.ShapeDtypeStruct((DEPTH, MOD_ROWS, n), F32),
        compiler_params=_cparams(("parallel", "parallel")),
        name="adaln",
    )(cvec, w_ada, b_ada.reshape(DEPTH, 1, n))
    return out.reshape(DEPTH, MOD_ROWS, N_MOD, D_MODEL)


def _mod_spec(layer, tiles_per_batch, n_batch):
    return pl.BlockSpec((None, None, N_MOD, D_MODEL),
                        lambda i, j: (layer, jnp.minimum(i // tiles_per_batch, n_batch), 0, 0))


def _ln_spec(layer, sub):
    return pl.BlockSpec((None, None, 1, D_MODEL), lambda i, j: (layer, sub, 0, 0))


def _ffn_kernel(x_ref, mod_ref, wg_ref, wu_ref, wd_ref, g_ref, b_ref, o_ref, h_ref, *, sub, nj):
    j = pl.program_id(1)

    @pl.when(j == 0)
    def _():
        shift = mod_ref[3 * sub:3 * sub + 1, :]
        scale = mod_ref[3 * sub + 1:3 * sub + 2, :]
        h_ref[...] = (x_ref[...] * (1.0 + scale) + shift).astype(BF16)
        o_ref[...] = jnp.zeros_like(o_ref)

    rows = h_ref.shape[0] // FFN_SPLIT
    for r in range(FFN_SPLIT):
        h = h_ref[r * rows:(r + 1) * rows, :]
        gate = _dot(h, wg_ref[...])
        up = _dot(h, wu_ref[...])
        act = (_silu(gate) * up).astype(BF16)
        o_ref[r * rows:(r + 1) * rows, :] += _dot(act, wd_ref[...])

    @pl.when(j == nj - 1)
    def _():
        gain = mod_ref[3 * sub + 2:3 * sub + 3, :]
        v = ALPHA * x_ref[...] + 0.5 * gain * o_ref[...]
        o_ref[...] = _layer_norm(v, g_ref[...], b_ref[...])


def _ffn(xs, rows, mods, wgu, wd, ln_g, ln_b, *, layer, which, sub, tiles_per_batch, n_batch):
    nj = D_FF // TF
    return pl.pallas_call(
        functools.partial(_ffn_kernel, sub=sub, nj=nj),
        grid=(rows // TM, nj),
        in_specs=[
            pl.BlockSpec((TM, D_MODEL), lambda i, j: (i, 0)),
            _mod_spec(layer, tiles_per_batch, n_batch),
            pl.BlockSpec((None, None, D_MODEL, TF), lambda i, j: (layer, which, 0, j)),
            pl.BlockSpec((None, None, D_MODEL, TF), lambda i, j: (layer, which, 0, nj + j)),
            pl.BlockSpec((None, None, TF, D_MODEL), lambda i, j: (layer, which, j, 0)),
            _ln_spec(layer, sub),
            _ln_spec(layer, sub),
        ],
        out_specs=pl.BlockSpec((TM, D_MODEL), lambda i, j: (i, 0)),
        out_shape=jax.ShapeDtypeStruct((rows, D_MODEL), F32),
        scratch_shapes=[pltpu.VMEM((TM, D_MODEL), BF16)],
        compiler_params=_cparams(("parallel", "arbitrary")),
        name="ffn",
    )(xs, mods, wgu, wgu, wd, ln_g, ln_b)


def _inproj_kernel(x_ref, mod_ref, w_ref, wdt_ref, o_ref, dt_ref, h_ref):
    j = pl.program_id(1)

    @pl.when(j == 0)
    def _():
        shift = mod_ref[3:4, :]
        scale = mod_ref[4:5, :]
        h = (x_ref[...] * (1.0 + scale) + shift).astype(BF16)
        h_ref[...] = h
        dt_ref[...] = _dot(h, wdt_ref[...])

    o_ref[...] = _dot(h_ref[...], w_ref[...]).astype(BF16)


def _inproj(xs, mods, w_main, w_dt, *, layer, tiles_per_batch, n_batch):
    rows = xs.shape[0]
    return pl.pallas_call(
        _inproj_kernel,
        grid=(rows // TM, PROJ_W // TN_PROJ),
        in_specs=[
            pl.BlockSpec((TM, D_MODEL), lambda i, j: (i, 0)),
            _mod_spec(layer, tiles_per_batch, n_batch),
            pl.BlockSpec((None, D_MODEL, TN_PROJ), lambda i, j: (layer, 0, j)),
            pl.BlockSpec((None, D_MODEL, DT_W), lambda i, j: (layer, 0, 0)),
        ],
        out_specs=[
            pl.BlockSpec((TM, TN_PROJ), lambda i, j: (i, j)),
            pl.BlockSpec((TM, DT_W), lambda i, j: (i, 0)),
        ],
        out_shape=[
            jax.ShapeDtypeStruct((rows, PROJ_W), BF16),
            jax.ShapeDtypeStruct((rows, DT_W), F32),
        ],
        scratch_shapes=[pltpu.VMEM((TM, D_MODEL), BF16)],
        compiler_params=_cparams(("parallel", "arbitrary")),
        name="inproj",
    )(xs, mods, w_main, w_dt)


def _conv_kernel(prev_ref, cur_ref, next_ref, w_ref, b_ref, o_ref, *, lat_tiles, tiles_per_seq):
    k = pl.program_id(0)
    in_ctx = k >= lat_tiles
    pos = k % tiles_per_seq
    first = jnp.logical_or(in_ctx, pos == 0)
    last = jnp.logical_or(in_ctx, pos == tiles_per_seq - 1)
    cur = cur_ref[...].astype(F32)
    n = cur.shape[0]
    prev = jnp.where(first, 0.0, prev_ref[...].astype(F32))
    nxt = jnp.where(last, 0.0, next_ref[...].astype(F32))
    row = lax.broadcasted_iota(jnp.int32, cur.shape, 0)
    w = w_ref[...]
    acc = cur * w[2:3, :] + b_ref[...]
    s = jnp.where(row == 0, prev[CONV_HALO - 1:CONV_HALO, :], pltpu.roll(cur, 1, 0))
    acc += s * w[1:2, :]
    s = jnp.where(row == 0, prev[CONV_HALO - 2:CONV_HALO - 1, :],
                  jnp.where(row == 1, prev[CONV_HALO - 1:CONV_HALO, :], pltpu.roll(cur, 2, 0)))
    acc += s * w[0:1, :]
    s = jnp.where(row == n - 1, nxt[0:1, :], pltpu.roll(cur, n - 1, 0))
    acc += s * w[3:4, :]
    s = jnp.where(row == n - 1, nxt[1:2, :],
                  jnp.where(row == n - 2, nxt[0:1, :], pltpu.roll(cur, n - 2, 0)))
    acc += s * w[4:5, :]
    o_ref[...] = _silu(acc).astype(BF16)


def _conv(proj, conv_w, conv_b, *, layer, lat_rows, seq_len):
    rows = proj.shape[0]
    halo_per_tile = CONV_ROWS // CONV_HALO
    n_halo = rows // CONV_HALO
    col0 = COL_XBC // CONV_DIM
    return pl.pallas_call(
        functools.partial(_conv_kernel, lat_tiles=lat_rows // CONV_ROWS, tiles_per_seq=seq_len // CONV_ROWS),
        grid=(rows // CONV_ROWS,),
        in_specs=[
            pl.BlockSpec((CONV_HALO, CONV_DIM), lambda k: (jnp.maximum(k * halo_per_tile - 1, 0), col0)),
            pl.BlockSpec((CONV_ROWS, CONV_DIM), lambda k: (k, col0)),
            pl.BlockSpec((CONV_HALO, CONV_DIM), lambda k: (jnp.minimum((k + 1) * halo_per_tile, n_halo - 1), col0)),
            pl.BlockSpec((None, D_CONV, CONV_DIM), lambda k: (layer, 0, 0)),
            pl.BlockSpec((None, 1, CONV_DIM), lambda k: (layer, 0, 0)),
        ],
        out_specs=pl.BlockSpec((CONV_ROWS, CONV_DIM), lambda k: (k, 0)),
        out_shape=jax.ShapeDtypeStruct((rows, CONV_DIM), BF16),
        compiler_params=_cparams(("parallel",)),
        name="conv",
    )(proj, proj, proj, conv_w, conv_b.reshape(DEPTH, 1, CONV_DIM))


def _qk_kernel(q_ref, k_ref, g_ref, cos_ref, sin_ref, o_ref, *, lat_tiles):
    is_lat = pl.program_id(0) < lat_tiles
    cos = jnp.where(is_lat, cos_ref[...], 1.0)
    sin = jnp.where(is_lat, sin_ref[...], 0.0)
    lane = lax.broadcasted_iota(jnp.int32, cos.shape, 1)
    half = AXIS_ROPE_DIM // 2
    first_half = (lane % AXIS_ROPE_DIM) < half
    q_scale = HEAD_DIM ** -0.5 * math.log2(math.e)
    for h in range(N_Q_HEADS + N_KV_HEADS):
        if h < N_Q_HEADS:
            x = q_ref[:, h * HEAD_DIM:(h + 1) * HEAD_DIM].astype(F32)
            gain = g_ref[0:1, :] * q_scale
        else:
            x = k_ref[:, (h - N_Q_HEADS) * HEAD_DIM:(h - N_Q_HEADS + 1) * HEAD_DIM].astype(F32)
            gain = g_ref[1:2, :]
        xn = x * lax.rsqrt(jnp.mean(x * x, axis=-1, keepdims=True) + RMS_EPS) * gain
        partner = jnp.where(first_half, pltpu.roll(xn, HEAD_DIM - half, 1), pltpu.roll(xn, half, 1))
        o_ref[:, h * HEAD_DIM:(h + 1) * HEAD_DIM] = (xn * cos + partner * sin).astype(BF16)


def _qk_prep(proj, gains, cos_t, sin_t, *, lat_rows, seq_len):
    rows = proj.shape[0]
    width = ATTN_DIM + KV_DIM
    tiles_per_seq = seq_len // TM
    return pl.pallas_call(
        functools.partial(_qk_kernel, lat_tiles=lat_rows // TM),
        grid=(rows // TM,),
        in_specs=[
            pl.BlockSpec((TM, ATTN_DIM), lambda i: (i, COL_Q // ATTN_DIM)),
            pl.BlockSpec((TM, KV_DIM), lambda i: (i, COL_K // KV_DIM)),
            pl.BlockSpec((2, HEAD_DIM), lambda i: (0, 0)),
            pl.BlockSpec((TM, HEAD_DIM), lambda i: (i % tiles_per_seq, 0)),
            pl.BlockSpec((TM, HEAD_DIM), lambda i: (i % tiles_per_seq, 0)),
        ],
        out_specs=pl.BlockSpec((TM, width), lambda i: (i, 0)),
        out_shape=jax.ShapeDtypeStruct((rows, width), BF16),
        compiler_params=_cparams(("parallel",)),
        name="qk_prep",
    )(proj, proj, gains, cos_t, sin_t)


def _rope_tables(seq_len):
    t = jnp.arange(seq_len)
    row = (t // GRID_W).astype(F32)
    col = (t % GRID_W).astype(F32)
    inv_freq = ROPE_THETA ** (-jnp.arange(0, AXIS_ROPE_DIM, 2, dtype=F32) / AXIS_ROPE_DIM)
    ang_r = row[:, None] * inv_freq
    ang_c = col[:, None] * inv_freq
    cos_t = jnp.concatenate([jnp.cos(ang_r), jnp.cos(ang_r), jnp.cos(ang_c), jnp.cos(ang_c)], axis=-1)
    sin_t = jnp.concatenate([-jnp.sin(ang_r), jnp.sin(ang_r), -jnp.sin(ang_c), jnp.sin(ang_c)], axis=-1)
    return cos_t, sin_t


def _with_ones(v):
    return jnp.concatenate([v, jnp.ones_like(v)], axis=1)


def _attn_kernel(q_ref, kl_ref, kc_ref, vl_ref, vc_ref, o_ref, *, nq):
    qi = pl.program_id(2)
    vc_ext = _with_ones(vc_ref[...])

    def finish(h, o_ext):
        o_ref[:, h * HEAD_DIM:(h + 1) * HEAD_DIM] = (o_ext[:, :HEAD_DIM] / o_ext[:, HEAD_DIM:]).astype(BF16)

    @pl.when(qi < nq)
    def _():
        vl_ext = _with_ones(vl_ref[...])
        for h in range(Q_PER_KV):
            q = q_ref[:, h * HEAD_DIM:(h + 1) * HEAD_DIM]
            s_l = _dot_nt(q, kl_ref[...])
            s_c = _dot_nt(q, kc_ref[...])
            m = jnp.maximum(jnp.max(s_l, axis=-1, keepdims=True), jnp.max(s_c, axis=-1, keepdims=True))
            p_l = jnp.exp2(s_l - m).astype(BF16)
            p_c = jnp.exp2(s_c - m).astype(BF16)
            finish(h, _dot(p_l, vl_ext) + _dot(p_c, vc_ext))

    @pl.when(qi == nq)
    def _():
        for h in range(Q_PER_KV):
            s_c = _dot_nt(q_ref[:, h * HEAD_DIM:(h + 1) * HEAD_DIM], kc_ref[...])
            p_c = jnp.exp2(s_c - jnp.max(s_c, axis=-1, keepdims=True)).astype(BF16)
            finish(h, _dot(p_c, vc_ext))


def _attention(qk, proj, *, n_batch, seq_len, ctx_len):
    rows = qk.shape[0]
    nq = seq_len // TQ
    ctx_blk0 = n_batch * seq_len // ctx_len
    kcol = N_Q_HEADS
    vcol = COL_V // HEAD_DIM
    qw = Q_PER_KV * HEAD_DIM

    def q_map(b, g, qi):
        return (jnp.where(qi < nq, b * nq + qi, n_batch * nq + b), g)

    return pl.pallas_call(
        functools.partial(_attn_kernel, nq=nq),
        grid=(n_batch, N_KV_HEADS, nq + 1),
        in_specs=[
            pl.BlockSpec((TQ, qw), q_map),
            pl.BlockSpec((seq_len, HEAD_DIM), lambda b, g, qi: (b, kcol + g)),
            pl.BlockSpec((ctx_len, HEAD_DIM), lambda b, g, qi: (ctx_blk0 + b, kcol + g)),
            pl.BlockSpec((seq_len, HEAD_DIM), lambda b, g, qi: (b, vcol + g)),
            pl.BlockSpec((ctx_len, HEAD_DIM), lambda b, g, qi: (ctx_blk0 + b, vcol + g)),
        ],
        out_specs=pl.BlockSpec((TQ, qw), q_map),
        out_shape=jax.ShapeDtypeStruct((rows, ATTN_DIM), BF16),
        compiler_params=_cparams(("parallel", "parallel", "arbitrary")),
        name="attention",
    )(qk, qk, qk, proj, proj)


def _ssd_kernel(x_ref, b_ref, c_ref, dt_ref, dtt_ref, bias_ref, biast_ref, alog_ref, alogt_ref,
                y_ref, st_ref):
    d = pl.program_id(1)
    c = pl.program_id(2)
    fwd = d == 0
    nh = SSD_HEADS
    gw = HEADS_PER_GROUP * SSD_HEAD_DIM
    n = SSD_CHUNK

    @pl.when(c == 0)
    def _():
        st_ref[...] = jnp.zeros_like(st_ref)

    row = lax.broadcasted_iota(jnp.int32, (n, n), 0)
    col = lax.broadcasted_iota(jnp.int32, (n, n), 1)
    sgn = jnp.where(fwd, 1, -1)
    mask = (row - col) * sgn >= 0
    tri = mask.astype(BF16)
    tri_t = ((col - row) * sgn >= 0).astype(BF16)

    dt_raw = dt_ref[...]
    dt_raw = jnp.where(fwd, dt_raw, pltpu.roll(dt_raw, DT_W - nh, 1))
    dtv = _softplus(dt_raw + bias_ref[...])
    da = dtv * (-jnp.exp(alog_ref[...]))
    da3 = _split3(da)
    cs = _dot(tri, da3[0]) + _dot(tri, da3[1]) + _dot(tri, da3[2])
    total = jnp.where(fwd, cs[n - 1:n, :], cs[0:1, :])

    start = pl.multiple_of(nh * d, nh)
    dtv_t = _softplus(dtt_ref[pl.ds(start, nh), :] + biast_ref[...])
    da_t = dtv_t * (-jnp.exp(alogt_ref[...]))
    da_t3 = _split3(da_t)
    cs_t = _dot(da_t3[0], tri_t) + _dot(da_t3[1], tri_t) + _dot(da_t3[2], tri_t)

    decay_end = jnp.exp(total - cs)
    ecs = jnp.exp(cs)
    lane = lax.broadcasted_iota(jnp.int32, (n, DT_W), 1)
    packed = jnp.where(lane < nh, dtv,
                       jnp.where(lane < 2 * nh, pltpu.roll(dtv * decay_end, nh, 1),
                                 jnp.where(lane < 3 * nh, pltpu.roll(ecs, 2 * nh, 1), 0.0)))
    er = lax.broadcasted_iota(jnp.int32, (DT_W, 3 * SSD_DIM), 0)
    ec = lax.broadcasted_iota(jnp.int32, (DT_W, 3 * SSD_DIM), 1)
    expand = (er == (ec // SSD_DIM) * nh + (ec % SSD_DIM) // SSD_HEAD_DIM).astype(BF16)
    p_hi, p_lo = _split2(packed)
    ex = _dot(p_hi, expand) + _dot(p_lo, expand)
    dt_x = ex[:, 0:SSD_DIM]
    dtdec_x = ex[:, SSD_DIM:2 * SSD_DIM]
    ecs_x = ex[:, 2 * SSD_DIM:3 * SSD_DIM]

    xf = x_ref[...].astype(F32)
    xdt = (xf * dt_x).astype(BF16)
    xdec = (xf * dtdec_x).astype(BF16)
    lane_h = lax.broadcasted_iota(jnp.int32, (n, 2 * SSD_HEAD_DIM), 1)

    for g in range(SSD_GROUPS):
        bg = b_ref[:, g * SSD_STATE:(g + 1) * SSD_STATE]
        cg = c_ref[:, g * SSD_STATE:(g + 1) * SSD_STATE]
        cb = _dot_nt(cg, bg)
        st = st_ref[g]
        y_off = _dot(cg, st.astype(BF16)) * ecs_x[:, g * gw:(g + 1) * gw]
        for j in range(HEADS_PER_GROUP // 2):
            ws = []
            for e in (2 * j, 2 * j + 1):
                hcol = g * HEADS_PER_GROUP + e
                diff = cs[:, hcol:hcol + 1] - cs_t[hcol:hcol + 1, :]
                lm = jnp.exp(jnp.where(mask, diff, -1e30))
                ws.append((cb * lm).astype(BF16))
            w_pair = jnp.concatenate(ws, axis=1)
            lo = g * gw + j * 2 * SSD_HEAD_DIM
            x2 = xdt[:, lo:lo + 2 * SSD_HEAD_DIM]
            rhs = jnp.concatenate([jnp.where(lane_h < SSD_HEAD_DIM, x2, jnp.zeros_like(x2)),
                                   jnp.where(lane_h >= SSD_HEAD_DIM, x2, jnp.zeros_like(x2))], axis=0)
            y_ref[:, lo:lo + 2 * SSD_HEAD_DIM] = (
                _dot(w_pair, rhs) + y_off[:, j * 2 * SSD_HEAD_DIM:(j + 1) * 2 * SSD_HEAD_DIM])
        new_states = _dot_tn(bg, xdec[:, g * gw:(g + 1) * gw])
        etot = jnp.where(fwd, ecs_x[n - 1:n, g * gw:(g + 1) * gw], ecs_x[0:1, g * gw:(g + 1) * gw])
        st_ref[g] = st * etot + new_states


def _ssd(xbc, dt, dt_t, bias, bias_t, alog, alog_t, *, layer, n_batch, seq_len, ctx_len):
    rows = xbc.shape[0]
    ncc = ctx_len // SSD_CHUNK
    ncl = seq_len // SSD_CHUNK
    lat_blk = seq_len // SSD_CHUNK
    ctx_blk0 = n_batch * seq_len // SSD_CHUNK

    def rb(b, d, c):
        cc = jnp.where(d == 0, c, ncc - 1 - c)
        lc = jnp.where(d == 0, c - ncc, ncl - 1 - (c - ncc))
        return jnp.where(c < ncc, ctx_blk0 + b * ncc + cc, b * lat_blk + lc)

    bcol = SSD_DIM // (SSD_GROUPS * SSD_STATE)
    return pl.pallas_call(
        _ssd_kernel,
        grid=(n_batch, 2, ncc + ncl),
        in_specs=[
            pl.BlockSpec((SSD_CHUNK, SSD_DIM), lambda b, d, c: (rb(b, d, c), 0)),
            pl.BlockSpec((SSD_CHUNK, SSD_GROUPS * SSD_STATE), lambda b, d, c: (rb(b, d, c), bcol)),
            pl.BlockSpec((SSD_CHUNK, SSD_GROUPS * SSD_STATE), lambda b, d, c: (rb(b, d, c), bcol + 1)),
            pl.BlockSpec((SSD_CHUNK, DT_W), lambda b, d, c: (rb(b, d, c), 0)),
            pl.BlockSpec((DT_W, SSD_CHUNK), lambda b, d, c: (0, rb(b, d, c))),
            pl.BlockSpec((None, None, 1, DT_W), lambda b, d, c: (layer, d, 0, 0)),
            pl.BlockSpec((None, None, SSD_HEADS, 1), lambda b, d, c: (layer, d, 0, 0)),
            pl.BlockSpec((None, None, 1, DT_W), lambda b, d, c: (layer, d, 0, 0)),
            pl.BlockSpec((None, None, SSD_HEADS, 1), lambda b, d, c: (layer, d, 0, 0)),
        ],
        out_specs=pl.BlockSpec((None, SSD_CHUNK, SSD_DIM), lambda b, d, c: (d, rb(b, d, c), 0)),
        out_shape=jax.ShapeDtypeStruct((2, rows, SSD_DIM), F32),
        scratch_shapes=[pltpu.VMEM((SSD_GROUPS, SSD_STATE, HEADS_PER_GROUP * SSD_HEAD_DIM), F32)],
        compiler_params=_cparams(("parallel", "parallel", "arbitrary")),
        name="ssd",
    )(xbc, xbc, xbc, dt, dt_t, bias, bias_t, alog, alog_t)


def _fourier_kernel(u0_ref, u1_ref, u2_ref, u3_ref, cs_ref, m_ref, *rest, seq_len, aliased):
    if aliased:
        _, o_ref, p_ref = rest
    else:
        o_ref, p_ref = rest
    mi = pl.program_id(1)
    gd = FOURIER_GROUP_DIM
    rows = min(seq_len, 512)

    @pl.when(mi == 0)
    def _():
        for g, u_ref in enumerate((u0_ref, u1_ref, u2_ref, u3_ref)):
            for r in range(seq_len // rows):
                ab = _dot(u_ref[r * rows:(r + 1) * rows, :], cs_ref[...])
                p_ref[r * rows:(r + 1) * rows, g * gd:(g + 1) * gd] = ab[:, :gd].astype(BF16)
                p_ref[seq_len + r * rows:seq_len + (r + 1) * rows, g * gd:(g + 1) * gd] = ab[:, gd:].astype(BF16)

    o_ref[...] = _dot(m_ref[...], p_ref[...]).astype(BF16)


def _fourier(proj, chan_mat, seq_mat, prev_out, *, n_batch, seq_len, row0, total_rows):
    tm = min(TM_FOURIER, seq_len)
    nm = seq_len // tm
    gd = FOURIER_GROUP_DIM
    col0 = COL_U // gd
    blk0 = row0 // seq_len
    oblk0 = row0 // tm
    aliased = prev_out is not None
    u_specs = [pl.BlockSpec((seq_len, gd), functools.partial(lambda b, mi, g: (blk0 + b, col0 + g), g=g))
               for g in range(FOURIER_GROUPS)]
    in_specs = u_specs + [
        pl.BlockSpec((gd, 2 * gd), lambda b, mi: (0, 0)),
        pl.BlockSpec((tm, 2 * seq_len), lambda b, mi: (mi, 0)),
    ]
    args = [proj, proj, proj, proj, chan_mat, seq_mat]
    if aliased:
        in_specs.append(pl.BlockSpec(memory_space=pl.ANY))
        args.append(prev_out)
    return pl.pallas_call(
        functools.partial(_fourier_kernel, seq_len=seq_len, aliased=aliased),
        grid=(n_batch, nm),
        in_specs=in_specs,
        out_specs=pl.BlockSpec((tm, FOURIER_DIM), lambda b, mi: (oblk0 + b * nm + mi, 0)),
        out_shape=jax.ShapeDtypeStruct((total_rows, FOURIER_DIM), BF16),
        scratch_shapes=[pltpu.VMEM((2 * seq_len, FOURIER_DIM), BF16)],
        input_output_aliases=({6: 0} if aliased else {}),
        compiler_params=_cparams(("parallel", "arbitrary")),
        name="fourier",
    )(*args)


def _dft_cos_sin(n):
    j = jnp.arange(n, dtype=jnp.int32)
    jk = (j[:, None] * j[None, :]) % n
    ang = jk.astype(F32) * (2.0 * math.pi / n)
    return jnp.cos(ang), jnp.sin(ang)


def _dft_tables(n):
    cos_m, sin_m = _dft_cos_sin(n)
    return (jnp.concatenate([cos_m, -sin_m], axis=1) * (n ** -0.5)).astype(BF16)


def _chan_table():
    cos_m, sin_m = _dft_cos_sin(FOURIER_GROUP_DIM)
    return (jnp.concatenate([cos_m, sin_m], axis=1) * (FOURIER_GROUP_DIM ** -0.5)).astype(BF16)


def _merge_kernel(x_ref, mod_ref, yf_ref, yb_ref, xh_ref, z_ref, four_ref, attn_ref, g0_ref, g1_ref, g2_ref,
                  wb0_ref, wb1_ref, wb2_ref, wo_ref, dsk_ref, ng_ref, lg_ref, lb_ref, o_ref, ys_ref, *, nj):
    j = pl.program_id(1)

    @pl.when(j == 0)
    def _():
        y = yf_ref[...] + yb_ref[...] + dsk_ref[...] * xh_ref[...].astype(F32)
        y = y * _silu(z_ref[...].astype(F32))
        gw = SSD_DIM // SSD_GROUPS
        for g in range(SSD_GROUPS):
            yg = y[:, g * gw:(g + 1) * gw]
            yg = yg * lax.rsqrt(jnp.mean(yg * yg, axis=-1, keepdims=True) + RMS_EPS)
            ys_ref[:, g * gw:(g + 1) * gw] = (yg * ng_ref[:, g * gw:(g + 1) * gw]).astype(BF16)
        o_ref[...] = jnp.zeros_like(o_ref)

    rows = o_ref.shape[0] // MERGE_SPLIT
    for r in range(MERGE_SPLIT):
        rs = slice(r * rows, (r + 1) * rows)
        m = (jax.nn.sigmoid(g0_ref[rs, :].astype(F32)) * _dot(ys_ref[rs, :], wb0_ref[...])
             + jax.nn.sigmoid(g1_ref[rs, :].astype(F32)) * _dot(four_ref[rs, :], wb1_ref[...])
             + jax.nn.sigmoid(g2_ref[rs, :].astype(F32)) * _dot(attn_ref[rs, :], wb2_ref[...]))
        o_ref[rs, :] += _dot(m.astype(BF16), wo_ref[...])

    @pl.when(j == nj - 1)
    def _():
        v = ALPHA * x_ref[...] + mod_ref[5:6, :] * o_ref[...]
        o_ref[...] = _layer_norm(v, lg_ref[...], lb_ref[...])


def _merge(xs, rows, mods, y_ssd, xbc, proj, four, attn, w_branch, w_out, dsk, norm_g, ln_g, ln_b, *,
           layer, tiles_per_batch, n_batch):
    nj = D_MODEL // TN_MERGE
    gcol = COL_G // TN_MERGE
    gstep = D_MODEL // TN_MERGE

    def gate_spec(k):
        return pl.BlockSpec((TM, TN_MERGE), lambda i, j: (i, gcol + k * gstep + j))

    def wb_spec(k):
        return pl.BlockSpec((None, None, BRANCH_DIM, TN_MERGE), lambda i, j: (layer, k, 0, j))

    row_spec = pl.BlockSpec((TM, BRANCH_DIM), lambda i, j: (i, 0))
    vec_spec = pl.BlockSpec((None, 1, BRANCH_DIM), lambda i, j: (layer, 0, 0))
    return pl.pallas_call(
        functools.partial(_merge_kernel, nj=nj),
        grid=(rows // TM, nj),
        in_specs=[
            pl.BlockSpec((TM, D_MODEL), lambda i, j: (i, 0)),
            _mod_spec(layer, tiles_per_batch, n_batch),
            pl.BlockSpec((None, TM, SSD_DIM), lambda i, j: (0, i, 0)),
            pl.BlockSpec((None, TM, SSD_DIM), lambda i, j: (1, i, 0)),
            row_spec,
            row_spec,
            row_spec,
            row_spec,
            gate_spec(0), gate_spec(1), gate_spec(2),
            wb_spec(0), wb_spec(1), wb_spec(2),
            pl.BlockSpec((None, TN_MERGE, D_MODEL), lambda i, j: (layer, j, 0)),
            vec_spec, vec_spec,
            _ln_spec(layer, 1), _ln_spec(layer, 1),
        ],
        out_specs=pl.BlockSpec((TM, D_MODEL), lambda i, j: (i, 0)),
        out_shape=jax.ShapeDtypeStruct((rows, D_MODEL), F32),
        scratch_shapes=[pltpu.VMEM((TM, SSD_DIM), BF16)],
        compiler_params=_cparams(("parallel", "arbitrary")),
        name="merge",
    )(xs, mods, y_ssd, y_ssd, xbc, proj, four, attn, proj, proj, proj,
      w_branch, w_branch, w_branch, w_out, dsk, norm_g, ln_g, ln_b)


def kernel(x, c, ctx, c_ctx, w_ada, b_ada, ln_g, ln_b, ffn_wgu, ffn_wd, w_in, conv_w, conv_b, dt_bias, a_log,
           d_skip, ssd_norm_g, q_norm_g, k_norm_g, w_branch, w_out):
    n_batch, seq_len, _ = x.shape
    ctx_len = ctx.shape[1]
    lat_rows = n_batch * seq_len
    rows = lat_rows + n_batch * ctx_len
    assert ctx_len == TQ and seq_len % TM == 0 and (n_batch * ctx_len) % TM == 0 and n_batch < MOD_ROWS
    tiles_per_batch = seq_len // TM
    tile_kw = dict(tiles_per_batch=tiles_per_batch, n_batch=n_batch)

    wgu_bf = ffn_wgu.astype(BF16)
    wd_bf = ffn_wd.astype(BF16)
    wb_bf = w_branch.astype(BF16)
    wo_bf = w_out.astype(BF16)
    w_z, w_xbc, w_dtf, w_dtb, w_u, w_q, w_k, w_v, w_g = jnp.split(w_in, IN_SPLITS, axis=-1)
    w_main = jnp.concatenate([w_z, w_u, w_q, w_xbc, w_k, w_v, w_g], axis=-1).astype(BF16)
    w_dt = jnp.concatenate([w_dtf, w_dtb, jnp.zeros((DEPTH, D_MODEL, DT_W - 2 * SSD_HEADS), F32)],
                           axis=-1).astype(BF16)
    head_pad = ((0, 0), (0, 0), (0, DT_W - SSD_HEADS))
    bias_row = jnp.pad(dt_bias, head_pad).reshape(DEPTH, 2, 1, DT_W)
    alog_row = jnp.pad(a_log, head_pad).reshape(DEPTH, 2, 1, DT_W)
    bias_col = dt_bias.reshape(DEPTH, 2, SSD_HEADS, 1)
    alog_col = a_log.reshape(DEPTH, 2, SSD_HEADS, 1)
    dsk_all = jnp.repeat(d_skip, SSD_HEAD_DIM, axis=-1).reshape(DEPTH, 1, SSD_DIM)
    norm_g_all = ssd_norm_g.reshape(DEPTH, 1, SSD_DIM)
    qk_gains = jnp.stack([q_norm_g, k_norm_g], axis=1)
    ln_g4 = ln_g.reshape(DEPTH, 3, 1, D_MODEL)
    ln_b4 = ln_b.reshape(DEPTH, 3, 1, D_MODEL)
    cos_t, sin_t = _rope_tables(seq_len)
    chan_mat = _chan_table()
    seq_mat_l = _dft_tables(seq_len)
    seq_mat_c = _dft_tables(ctx_len)

    cvec = jnp.concatenate([c, c_ctx[None], jnp.zeros((MOD_ROWS - n_batch - 1, D_MODEL), F32)], axis=0)
    mods = _adaln_all(cvec, w_ada, b_ada)

    xs = jnp.concatenate([x.reshape(lat_rows, D_MODEL), ctx.reshape(n_batch * ctx_len, D_MODEL)], axis=0)
    for layer in range(DEPTH):
        last = layer == DEPTH - 1
        xs = _ffn(xs, rows, mods, wgu_bf, wd_bf, ln_g4, ln_b4, layer=layer, which=0, sub=0, **tile_kw)

        proj, dt = _inproj(xs, mods, w_main, w_dt, layer=layer, **tile_kw)
        xbc = _conv(proj, conv_w, conv_b, layer=layer, lat_rows=lat_rows, seq_len=seq_len)
        y_ssd = _ssd(xbc, dt, dt.T, bias_row, bias_col, alog_row, alog_col,
                     layer=layer, n_batch=n_batch, seq_len=seq_len, ctx_len=ctx_len)
        qk = _qk_prep(proj, qk_gains[layer], cos_t, sin_t, lat_rows=lat_rows, seq_len=seq_len)
        attn = _attention(qk, proj, n_batch=n_batch, seq_len=seq_len, ctx_len=ctx_len)
        four = _fourier(proj, chan_mat, seq_mat_l, None, n_batch=n_batch, seq_len=seq_len, row0=0,
                        total_rows=rows)
        if not last:
            four = _fourier(proj, chan_mat, seq_mat_c, four, n_batch=n_batch, seq_len=ctx_len, row0=lat_rows,
                            total_rows=rows)
        out_rows = lat_rows if last else rows
        xs = _merge(xs, out_rows, mods, y_ssd, xbc, proj, four, attn, wb_bf, wo_bf, dsk_all, norm_g_all,
                    ln_g4, ln_b4, layer=layer, **tile_kw)
        xs = _ffn(xs, out_rows, mods, wgu_bf, wd_bf, ln_g4, ln_b4, layer=layer, which=1, sub=2, **tile_kw)
    return xs.reshape(n_batch, seq_len, D_MODEL)
```

```python
import functools
import math

import jax
import jax.numpy as jnp
from jax import lax
from jax.experimental import pallas as pl
from jax.experimental.pallas import tpu as pltpu

F32 = jnp.float32
BF16 = jnp.bfloat16

D_MODEL = 2048
DEPTH = 4
GRID_W = 64
N_MOD = 9
ALPHA = (2 * DEPTH) ** 0.25
LN_EPS = 1e-6
RMS_EPS = 1e-6
D_FF = 5632
BRANCH_DIM = D_MODEL // 2
N_BRANCH = 3
SSD_DIM = BRANCH_DIM
SSD_HEAD_DIM = 64
SSD_HEADS = SSD_DIM // SSD_HEAD_DIM
SSD_GROUPS = 2
HEADS_PER_GROUP = SSD_HEADS // SSD_GROUPS
SSD_STATE = 128
SSD_CHUNK = 128
D_CONV = 5
CONV_DIM = SSD_DIM + 2 * SSD_GROUPS * SSD_STATE
FOURIER_DIM = BRANCH_DIM
FOURIER_GROUPS = 4
FOURIER_GROUP_DIM = FOURIER_DIM // FOURIER_GROUPS
HEAD_DIM = 128
N_Q_HEADS = BRANCH_DIM // HEAD_DIM
N_KV_HEADS = 2
Q_PER_KV = N_Q_HEADS // N_KV_HEADS
ATTN_DIM = N_Q_HEADS * HEAD_DIM
KV_DIM = N_KV_HEADS * HEAD_DIM
AXIS_ROPE_DIM = HEAD_DIM // 2
ROPE_THETA = 10000.0
IN_WIDTHS = (SSD_DIM, CONV_DIM, SSD_HEADS, SSD_HEADS, FOURIER_DIM, ATTN_DIM, KV_DIM, KV_DIM, N_BRANCH * D_MODEL)
IN_SPLITS = tuple(sum(IN_WIDTHS[:i + 1]) for i in range(len(IN_WIDTHS) - 1))

COL_G = 0
COL_Z = COL_G + N_BRANCH * D_MODEL
COL_U = COL_Z + SSD_DIM
COL_Q = COL_U + FOURIER_DIM
COL_XBC = COL_Q + ATTN_DIM
COL_K = COL_XBC + CONV_DIM
COL_V = COL_K + KV_DIM
PROJ_W = COL_V + KV_DIM
DT_W = 128
MOD_ROWS = 16

TM = 512
TF = 512
TN_PROJ = 2816
FFN_SPLIT = 2
TM_MERGE = 256
TN_MERGE = 512
TN_ADA = 1024
CONV_ROWS = 256
CONV_HALO = 16
TQ = 256
TK = 512
TM_FOURIER = 256
VMEM_LIMIT = 56 * 1024 * 1024


def _cparams(sem):
    return pltpu.CompilerParams(dimension_semantics=sem, vmem_limit_bytes=VMEM_LIMIT)


def _silu(x):
    return x * jax.nn.sigmoid(x)


def _softplus(x):
    return jnp.maximum(x, 0.0) + jnp.log1p(jnp.exp(-jnp.abs(x)))


def _layer_norm(v, g, b):
    mu = jnp.mean(v, axis=-1, keepdims=True)
    d = v - mu
    var = jnp.mean(d * d, axis=-1, keepdims=True)
    return d * lax.rsqrt(var + LN_EPS) * g + b


def _dot(a, b):
    return jnp.dot(a, b, preferred_element_type=F32)


def _dot_nt(a, b):
    return lax.dot_general(a, b, (((1,), (1,)), ((), ())), preferred_element_type=F32)


def _dot_tn(a, b):
    return lax.dot_general(a, b, (((0,), (0,)), ((), ())), preferred_element_type=F32)


def _split2(a):
    hi = a.astype(BF16)
    lo = (a - hi.astype(F32)).astype(BF16)
    return hi, lo


def _split3(a):
    hi = a.astype(BF16)
    r = a - hi.astype(F32)
    mid = r.astype(BF16)
    lo = (r - mid.astype(F32)).astype(BF16)
    return hi, mid, lo


def _ada_kernel(c_ref, w_ref, b_ref, o_ref):
    h = _silu(c_ref[...]).astype(BF16)
    o_ref[...] = _dot(h, w_ref[...].astype(BF16)) + b_ref[...]


def _adaln_all(cvec, w_ada, b_ada):
    n = N_MOD * D_MODEL
    out = pl.pallas_call(
        _ada_kernel,
        grid=(DEPTH, n // TN_ADA),
        in_specs=[
            pl.BlockSpec((MOD_ROWS, D_MODEL), lambda l, j: (0, 0)),
            pl.BlockSpec((None, D_MODEL, TN_ADA), lambda l, j: (l, 0, j)),
            pl.BlockSpec((None, 1, TN_ADA), lambda l, j: (l, 0, j)),
        ],
        out_specs=pl.BlockSpec((None, MOD_ROWS, TN_ADA), lambda l, j: (l, 0, j)),
        out_shape=jax.ShapeDtypeStruct((DEPTH, MOD_ROWS, n), F32),
        compiler_params=_cparams(("parallel", "parallel")),
        name="adaln",
    )(cvec, w_ada, b_ada.reshape(DEPTH, 1, n))
    return out.reshape(DEPTH, MOD_ROWS, N_MOD, D_MODEL)


def _mod_spec(layer, tiles_per_batch, n_batch):
    return pl.BlockSpec((None, None, N_MOD, D_MODEL),
                        lambda i, j: (layer, jnp.minimum(i // tiles_per_batch, n_batch), 0, 0))


def _ln_spec(layer, sub):
    return pl.BlockSpec((None, None, 1, D_MODEL), lambda i, j: (layer, sub, 0, 0))


def _ffn_kernel(x_ref, mod_ref, wgu_ref, wd_ref, g_ref, b_ref, o_ref, h_ref, *, sub, nj):
    j = pl.program_id(1)

    @pl.when(j == 0)
    def _():
        shift = mod_ref[3 * sub:3 * sub + 1, :]
        scale = mod_ref[3 * sub + 1:3 * sub + 2, :]
        h_ref[...] = (x_ref[...] * (1.0 + scale) + shift).astype(BF16)
        o_ref[...] = jnp.zeros_like(o_ref)

    rows = h_ref.shape[0] // FFN_SPLIT
    for r in range(FFN_SPLIT):
        h = h_ref[r * rows:(r + 1) * rows, :]
        gate_up = _dot(h, wgu_ref[...])
        act = (_silu(gate_up[:, :TF]) * gate_up[:, TF:]).astype(BF16)
        o_ref[r * rows:(r + 1) * rows, :] += _dot(act, wd_ref[...])

    @pl.when(j == nj - 1)
    def _():
        gain = mod_ref[3 * sub + 2:3 * sub + 3, :]
        v = ALPHA * x_ref[...] + 0.5 * gain * o_ref[...]
        o_ref[...] = _layer_norm(v, g_ref[...], b_ref[...])


def _ffn(xs, rows, mods, wgu, wd, ln_g, ln_b, *, layer, which, sub, tiles_per_batch, n_batch):
    nj = D_FF // TF
    return pl.pallas_call(
        functools.partial(_ffn_kernel, sub=sub, nj=nj),
        grid=(rows // TM, nj),
        in_specs=[
            pl.BlockSpec((TM, D_MODEL), lambda i, j: (i, 0)),
            _mod_spec(layer, tiles_per_batch, n_batch),
            pl.BlockSpec((None, None, None, D_MODEL, 2 * TF), lambda i, j: (layer, which, j, 0, 0)),
            pl.BlockSpec((None, None, TF, D_MODEL), lambda i, j: (layer, which, j, 0)),
            _ln_spec(layer, sub),
            _ln_spec(layer, sub),
        ],
        out_specs=pl.BlockSpec((TM, D_MODEL), lambda i, j: (i, 0)),
        out_shape=jax.ShapeDtypeStruct((rows, D_MODEL), F32),
        scratch_shapes=[pltpu.VMEM((TM, D_MODEL), BF16)],
        compiler_params=_cparams(("parallel", "arbitrary")),
        name="ffn",
    )(xs, mods, wgu, wd, ln_g, ln_b)


def _inproj_kernel(x_ref, mod_ref, w_ref, wdt_ref, o_ref, dt_ref, h_ref):
    j = pl.program_id(1)

    @pl.when(j == 0)
    def _():
        shift = mod_ref[3:4, :]
        scale = mod_ref[4:5, :]
        h = (x_ref[...] * (1.0 + scale) + shift).astype(BF16)
        h_ref[...] = h
        dt_ref[...] = _dot(h, wdt_ref[...])

    o_ref[...] = _dot(h_ref[...], w_ref[...]).astype(BF16)


def _inproj(xs, mods, w_main, w_dt, *, layer, tiles_per_batch, n_batch):
    rows = xs.shape[0]
    return pl.pallas_call(
        _inproj_kernel,
        grid=(rows // TM, PROJ_W // TN_PROJ),
        in_specs=[
            pl.BlockSpec((TM, D_MODEL), lambda i, j: (i, 0)),
            _mod_spec(layer, tiles_per_batch, n_batch),
            pl.BlockSpec((None, None, D_MODEL, TN_PROJ), lambda i, j: (layer, j, 0, 0)),
            pl.BlockSpec((None, D_MODEL, DT_W), lambda i, j: (layer, 0, 0)),
        ],
        out_specs=[
            pl.BlockSpec((TM, TN_PROJ), lambda i, j: (i, j)),
            pl.BlockSpec((TM, DT_W), lambda i, j: (i, 0)),
        ],
        out_shape=[
            jax.ShapeDtypeStruct((rows, PROJ_W), BF16),
            jax.ShapeDtypeStruct((rows, DT_W), F32),
        ],
        scratch_shapes=[pltpu.VMEM((TM, D_MODEL), BF16)],
        compiler_params=_cparams(("parallel", "arbitrary")),
        name="inproj",
    )(xs, mods, w_main, w_dt)


def _conv_kernel(prev_ref, cur_ref, next_ref, w_ref, b_ref, o_ref, *, lat_tiles, tiles_per_seq):
    k = pl.program_id(0)
    in_ctx = k >= lat_tiles
    pos = k % tiles_per_seq
    first = jnp.logical_or(in_ctx, pos == 0)
    last = jnp.logical_or(in_ctx, pos == tiles_per_seq - 1)
    cur = cur_ref[...].astype(F32)
    n = cur.shape[0]
    prev = jnp.where(first, 0.0, prev_ref[...].astype(F32))
    nxt = jnp.where(last, 0.0, next_ref[...].astype(F32))
    row = lax.broadcasted_iota(jnp.int32, cur.shape, 0)
    w = w_ref[...]
    acc = cur * w[2:3, :] + b_ref[...]
    s = jnp.where(row == 0, prev[CONV_HALO - 1:CONV_HALO, :], pltpu.roll(cur, 1, 0))
    acc += s * w[1:2, :]
    s = jnp.where(row == 0, prev[CONV_HALO - 2:CONV_HALO - 1, :],
                  jnp.where(row == 1, prev[CONV_HALO - 1:CONV_HALO, :], pltpu.roll(cur, 2, 0)))
    acc += s * w[0:1, :]
    s = jnp.where(row == n - 1, nxt[0:1, :], pltpu.roll(cur, n - 1, 0))
    acc += s * w[3:4, :]
    s = jnp.where(row == n - 1, nxt[1:2, :],
                  jnp.where(row == n - 2, nxt[0:1, :], pltpu.roll(cur, n - 2, 0)))
    acc += s * w[4:5, :]
    o_ref[...] = _silu(acc).astype(BF16)


def _conv(proj, conv_w, conv_b, *, layer, lat_rows, seq_len):
    rows = proj.shape[0]
    halo_per_tile = CONV_ROWS // CONV_HALO
    n_halo = rows // CONV_HALO
    col0 = COL_XBC // CONV_DIM
    return pl.pallas_call(
        functools.partial(_conv_kernel, lat_tiles=lat_rows // CONV_ROWS, tiles_per_seq=seq_len // CONV_ROWS),
        grid=(rows // CONV_ROWS,),
        in_specs=[
            pl.BlockSpec((CONV_HALO, CONV_DIM), lambda k: (jnp.maximum(k * halo_per_tile - 1, 0), col0)),
            pl.BlockSpec((CONV_ROWS, CONV_DIM), lambda k: (k, col0)),
            pl.BlockSpec((CONV_HALO, CONV_DIM), lambda k: (jnp.minimum((k + 1) * halo_per_tile, n_halo - 1), col0)),
            pl.BlockSpec((None, D_CONV, CONV_DIM), lambda k: (layer, 0, 0)),
            pl.BlockSpec((None, 1, CONV_DIM), lambda k: (layer, 0, 0)),
        ],
        out_specs=pl.BlockSpec((CONV_ROWS, CONV_DIM), lambda k: (k, 0)),
        out_shape=jax.ShapeDtypeStruct((rows, CONV_DIM), BF16),
        compiler_params=_cparams(("parallel",)),
        name="conv",
    )(proj, proj, proj, conv_w, conv_b.reshape(DEPTH, 1, CONV_DIM))


def _qk_kernel(q_ref, k_ref, g_ref, cos_ref, sin_ref, o_ref, *, lat_tiles):
    is_lat = pl.program_id(0) < lat_tiles
    cos = jnp.where(is_lat, cos_ref[...], 1.0)
    sin = jnp.where(is_lat, sin_ref[...], 0.0)
    lane = lax.broadcasted_iota(jnp.int32, cos.shape, 1)
    half = AXIS_ROPE_DIM // 2
    first_half = (lane % AXIS_ROPE_DIM) < half
    q_scale = HEAD_DIM ** -0.5 * math.log2(math.e)
    for h in range(N_Q_HEADS + N_KV_HEADS):
        if h < N_Q_HEADS:
            x = q_ref[:, h * HEAD_DIM:(h + 1) * HEAD_DIM].astype(F32)
            gain = g_ref[0:1, :] * q_scale
        else:
            x = k_ref[:, (h - N_Q_HEADS) * HEAD_DIM:(h - N_Q_HEADS + 1) * HEAD_DIM].astype(F32)
            gain = g_ref[1:2, :]
        xn = x * lax.rsqrt(jnp.mean(x * x, axis=-1, keepdims=True) + RMS_EPS) * gain
        partner = jnp.where(first_half, pltpu.roll(xn, HEAD_DIM - half, 1), pltpu.roll(xn, half, 1))
        o_ref[:, h * HEAD_DIM:(h + 1) * HEAD_DIM] = (xn * cos + partner * sin).astype(BF16)


def _qk_prep(proj, gains, cos_t, sin_t, *, lat_rows, seq_len):
    rows = proj.shape[0]
    width = ATTN_DIM + KV_DIM
    tiles_per_seq = seq_len // TM
    return pl.pallas_call(
        functools.partial(_qk_kernel, lat_tiles=lat_rows // TM),
        grid=(rows // TM,),
        in_specs=[
            pl.BlockSpec((TM, ATTN_DIM), lambda i: (i, COL_Q // ATTN_DIM)),
            pl.BlockSpec((TM, KV_DIM), lambda i: (i, COL_K // KV_DIM)),
            pl.BlockSpec((2, HEAD_DIM), lambda i: (0, 0)),
            pl.BlockSpec((TM, HEAD_DIM), lambda i: (i % tiles_per_seq, 0)),
            pl.BlockSpec((TM, HEAD_DIM), lambda i: (i % tiles_per_seq, 0)),
        ],
        out_specs=pl.BlockSpec((TM, width), lambda i: (i, 0)),
        out_shape=jax.ShapeDtypeStruct((rows, width), BF16),
        compiler_params=_cparams(("parallel",)),
        name="qk_prep",
    )(proj, proj, gains, cos_t, sin_t)


def _rope_tables(seq_len):
    t = jnp.arange(seq_len)
    row = (t // GRID_W).astype(F32)
    col = (t % GRID_W).astype(F32)
    inv_freq = ROPE_THETA ** (-jnp.arange(0, AXIS_ROPE_DIM, 2, dtype=F32) / AXIS_ROPE_DIM)
    ang_r = row[:, None] * inv_freq
    ang_c = col[:, None] * inv_freq
    cos_t = jnp.concatenate([jnp.cos(ang_r), jnp.cos(ang_r), jnp.cos(ang_c), jnp.cos(ang_c)], axis=-1)
    sin_t = jnp.concatenate([-jnp.sin(ang_r), jnp.sin(ang_r), -jnp.sin(ang_c), jnp.sin(ang_c)], axis=-1)
    return cos_t, sin_t


def _with_ones(v):
    return jnp.concatenate([v, jnp.ones_like(v)], axis=1)


def _attn_kernel(q_ref, kl_ref, kc_ref, vl_ref, vc_ref, o_ref, s_ref, ve_ref, *, nq, ctx_len, seq_len):
    qi = pl.program_id(2)
    chunks = [(0, ctx_len)] + [(ctx_len + c * TK, TK) for c in range(seq_len // TK)]

    @pl.when(qi == 0)
    def _():
        ve_ref[0:ctx_len, :] = _with_ones(vc_ref[...])
        ve_ref[ctx_len:, :] = _with_ones(vl_ref[...])

    def keys(lo, n):
        return kc_ref[...] if lo == 0 else kl_ref[lo - ctx_len:lo - ctx_len + n, :]

    def scores(h, lo, n, m_lanes):
        s = _dot_nt(q_ref[:, h * HEAD_DIM:(h + 1) * HEAD_DIM], keys(lo, n))
        s_ref[h % 2, :, lo:lo + n] = s
        for k in range(n // HEAD_DIM):
            blk = s[:, k * HEAD_DIM:(k + 1) * HEAD_DIM]
            m_lanes = blk if m_lanes is None else jnp.maximum(m_lanes, blk)
        return m_lanes

    def weighted(h, lo, n, m, o_ext):
        p = jnp.exp2(s_ref[h % 2, :, lo:lo + n] - m).astype(BF16)
        part = _dot(p, ve_ref[lo:lo + n, :])
        return part if o_ext is None else o_ext + part

    def finish(h, o_ext):
        o_ref[:, h * HEAD_DIM:(h + 1) * HEAD_DIM] = (o_ext[:, :HEAD_DIM] / o_ext[:, HEAD_DIM:]).astype(BF16)

    def run(chunk_list):
        m_lanes = None
        for lo, n in chunk_list:
            m_lanes = scores(0, lo, n, m_lanes)
        for h in range(Q_PER_KV):
            m = jnp.max(m_lanes, axis=-1, keepdims=True)
            m_lanes, o_ext = None, None
            for lo, n in chunk_list:
                o_ext = weighted(h, lo, n, m, o_ext)
                if h + 1 < Q_PER_KV:
                    m_lanes = scores(h + 1, lo, n, m_lanes)
            finish(h, o_ext)

    @pl.when(qi < nq)
    def _():
        run(chunks)

    @pl.when(qi == nq)
    def _():
        run(chunks[:1])


def _attention(qk, proj, *, n_batch, seq_len, ctx_len):
    rows = qk.shape[0]
    nq = seq_len // TQ
    ctx_blk0 = n_batch * seq_len // ctx_len
    kcol = N_Q_HEADS
    vcol = COL_V // HEAD_DIM
    qw = Q_PER_KV * HEAD_DIM

    def q_map(b, g, qi):
        return (jnp.where(qi < nq, b * nq + qi, n_batch * nq + b), g)

    return pl.pallas_call(
        functools.partial(_attn_kernel, nq=nq, ctx_len=ctx_len, seq_len=seq_len),
        grid=(n_batch, N_KV_HEADS, nq + 1),
        in_specs=[
            pl.BlockSpec((TQ, qw), q_map),
            pl.BlockSpec((seq_len, HEAD_DIM), lambda b, g, qi: (b, kcol + g)),
            pl.BlockSpec((ctx_len, HEAD_DIM), lambda b, g, qi: (ctx_blk0 + b, kcol + g)),
            pl.BlockSpec((seq_len, HEAD_DIM), lambda b, g, qi: (b, vcol + g)),
            pl.BlockSpec((ctx_len, HEAD_DIM), lambda b, g, qi: (ctx_blk0 + b, vcol + g)),
        ],
        out_specs=pl.BlockSpec((TQ, qw), q_map),
        out_shape=jax.ShapeDtypeStruct((rows, ATTN_DIM), BF16),
        scratch_shapes=[pltpu.VMEM((2, TQ, ctx_len + seq_len), F32),
                        pltpu.VMEM((ctx_len + seq_len, 2 * HEAD_DIM), BF16)],
        compiler_params=_cparams(("parallel", "parallel", "arbitrary")),
        name="attention",
    )(qk, qk, qk, proj, proj)


def _ssd_kernel(x_ref, b_ref, c_ref, dt_ref, dtt_ref, bias_ref, biast_ref, alog_ref, alogt_ref,
                y_ref, st_ref):
    d = pl.program_id(1)
    c = pl.program_id(2)
    fwd = d == 0
    nh = SSD_HEADS
    gw = HEADS_PER_GROUP * SSD_HEAD_DIM
    n = SSD_CHUNK

    @pl.when(c == 0)
    def _():
        st_ref[...] = jnp.zeros_like(st_ref)

    row = lax.broadcasted_iota(jnp.int32, (n, n), 0)
    col = lax.broadcasted_iota(jnp.int32, (n, n), 1)
    sgn = jnp.where(fwd, 1, -1)
    mask = (row - col) * sgn >= 0
    tri = mask.astype(BF16)
    tri_t = ((col - row) * sgn >= 0).astype(BF16)

    dt_raw = dt_ref[...]
    dt_raw = jnp.where(fwd, dt_raw, pltpu.roll(dt_raw, DT_W - nh, 1))
    dtv = _softplus(dt_raw + bias_ref[...])
    da = dtv * (-jnp.exp(alog_ref[...]))
    da3 = _split3(da)
    cs = _dot(tri, da3[0]) + _dot(tri, da3[1]) + _dot(tri, da3[2])
    total = jnp.where(fwd, cs[n - 1:n, :], cs[0:1, :])

    start = pl.multiple_of(nh * d, nh)
    dtv_t = _softplus(dtt_ref[pl.ds(start, nh), :] + biast_ref[...])
    da_t = dtv_t * (-jnp.exp(alogt_ref[...]))
    da_t3 = _split3(da_t)
    cs_t = _dot(da_t3[0], tri_t) + _dot(da_t3[1], tri_t) + _dot(da_t3[2], tri_t)

    decay_end = jnp.exp(total - cs)
    ecs = jnp.exp(cs)
    lane = lax.broadcasted_iota(jnp.int32, (n, DT_W), 1)
    packed = jnp.where(lane < nh, dtv,
                       jnp.where(lane < 2 * nh, pltpu.roll(dtv * decay_end, nh, 1),
                                 jnp.where(lane < 3 * nh, pltpu.roll(ecs, 2 * nh, 1), 0.0)))
    er = lax.broadcasted_iota(jnp.int32, (DT_W, 3 * SSD_DIM), 0)
    ec = lax.broadcasted_iota(jnp.int32, (DT_W, 3 * SSD_DIM), 1)
    expand = (er == (ec // SSD_DIM) * nh + (ec % SSD_DIM) // SSD_HEAD_DIM).astype(BF16)
    p_hi, p_lo = _split2(packed)
    ex = _dot(p_hi, expand) + _dot(p_lo, expand)
    dt_x = ex[:, 0:SSD_DIM]
    dtdec_x = ex[:, SSD_DIM:2 * SSD_DIM]
    ecs_x = ex[:, 2 * SSD_DIM:3 * SSD_DIM]

    xf = x_ref[...].astype(F32)
    xdt = (xf * dt_x).astype(BF16)
    xdec = (xf * dtdec_x).astype(BF16)
    lane_h = lax.broadcasted_iota(jnp.int32, (n, 2 * SSD_HEAD_DIM), 1)

    for g in range(SSD_GROUPS):
        bg = b_ref[:, g * SSD_STATE:(g + 1) * SSD_STATE]
        cg = c_ref[:, g * SSD_STATE:(g + 1) * SSD_STATE]
        cb = _dot_nt(cg, bg)
        st = st_ref[g]
        y_off = _dot(cg, st.astype(BF16)) * ecs_x[:, g * gw:(g + 1) * gw]
        for j in range(HEADS_PER_GROUP // 2):
            ws = []
            for e in (2 * j, 2 * j + 1):
                hcol = g * HEADS_PER_GROUP + e
                diff = cs[:, hcol:hcol + 1] - cs_t[hcol:hcol + 1, :]
                lm = jnp.exp(jnp.where(mask, diff, -1e30))
                ws.append((cb * lm).astype(BF16))
            w_pair = jnp.concatenate(ws, axis=1)
            lo = g * gw + j * 2 * SSD_HEAD_DIM
            x2 = xdt[:, lo:lo + 2 * SSD_HEAD_DIM]
            rhs = jnp.concatenate([jnp.where(lane_h < SSD_HEAD_DIM, x2, jnp.zeros_like(x2)),
                                   jnp.where(lane_h >= SSD_HEAD_DIM, x2, jnp.zeros_like(x2))], axis=0)
            y_ref[:, lo:lo + 2 * SSD_HEAD_DIM] = (
                _dot(w_pair, rhs) + y_off[:, j * 2 * SSD_HEAD_DIM:(j + 1) * 2 * SSD_HEAD_DIM])
        new_states = _dot_tn(bg, xdec[:, g * gw:(g + 1) * gw])
        etot = jnp.where(fwd, ecs_x[n - 1:n, g * gw:(g + 1) * gw], ecs_x[0:1, g * gw:(g + 1) * gw])
        st_ref[g] = st * etot + new_states


def _ssd(xbc, dt, dt_t, bias, bias_t, alog, alog_t, *, layer, n_batch, seq_len, ctx_len):
    rows = xbc.shape[0]
    ncc = ctx_len // SSD_CHUNK
    ncl = seq_len // SSD_CHUNK
    lat_blk = seq_len // SSD_CHUNK
    ctx_blk0 = n_batch * seq_len // SSD_CHUNK

    def rb(b, d, c):
        cc = jnp.where(d == 0, c, ncc - 1 - c)
        lc = jnp.where(d == 0, c - ncc, ncl - 1 - (c - ncc))
        return jnp.where(c < ncc, ctx_blk0 + b * ncc + cc, b * lat_blk + lc)

    bcol = SSD_DIM // (SSD_GROUPS * SSD_STATE)
    return pl.pallas_call(
        _ssd_kernel,
        grid=(n_batch, 2, ncc + ncl),
        in_specs=[
            pl.BlockSpec((SSD_CHUNK, SSD_DIM), lambda b, d, c: (rb(b, d, c), 0)),
            pl.BlockSpec((SSD_CHUNK, SSD_GROUPS * SSD_STATE), lambda b, d, c: (rb(b, d, c), bcol)),
            pl.BlockSpec((SSD_CHUNK, SSD_GROUPS * SSD_STATE), lambda b, d, c: (rb(b, d, c), bcol + 1)),
            pl.BlockSpec((SSD_CHUNK, DT_W), lambda b, d, c: (rb(b, d, c), 0)),
            pl.BlockSpec((DT_W, SSD_CHUNK), lambda b, d, c: (0, rb(b, d, c))),
            pl.BlockSpec((None, None, 1, DT_W), lambda b, d, c: (layer, d, 0, 0)),
            pl.BlockSpec((None, None, SSD_HEADS, 1), lambda b, d, c: (layer, d, 0, 0)),
            pl.BlockSpec((None, None, 1, DT_W), lambda b, d, c: (layer, d, 0, 0)),
            pl.BlockSpec((None, None, SSD_HEADS, 1), lambda b, d, c: (layer, d, 0, 0)),
        ],
        out_specs=pl.BlockSpec((None, SSD_CHUNK, SSD_DIM), lambda b, d, c: (d, rb(b, d, c), 0)),
        out_shape=jax.ShapeDtypeStruct((2, rows, SSD_DIM), F32),
        scratch_shapes=[pltpu.VMEM((SSD_GROUPS, SSD_STATE, HEADS_PER_GROUP * SSD_HEAD_DIM), F32)],
        compiler_params=_cparams(("parallel", "parallel", "arbitrary")),
        name="ssd",
    )(xbc, xbc, xbc, dt, dt_t, bias, bias_t, alog, alog_t)


def _fourier_fold(u_refs, chan_ref, p_ref, ah_ref, n):
    gd = FOURIER_GROUP_DIM
    half = n // 2
    bs = min(256, half)
    nb = half // bs
    r_i = lax.broadcasted_iota(jnp.int32, (bs, bs), 0)
    c_i = lax.broadcasted_iota(jnp.int32, (bs, bs), 1)
    flip = (r_i + c_i == bs).astype(BF16)
    row0 = lax.broadcasted_iota(jnp.int32, (bs, gd), 0) == 0
    cos_c = chan_ref[:, :gd]
    sin_c = chan_ref[:, gd:]
    for g, u_ref in enumerate(u_refs):
        cols = slice(g * gd, (g + 1) * gd)
        for i in range(nb):
            lo = u_ref[bs * i:bs * (i + 1), :].astype(F32)
            rev = _dot(flip, u_ref[bs * (2 * nb - 1 - i):bs * (2 * nb - i), :])
            if i > 0:
                first = u_ref[bs * (2 * nb - i):bs * (2 * nb - i) + 16, :][0:1, :].astype(F32)
                rev = jnp.where(row0, first, rev)
            p_ref[bs * i:bs * (i + 1), cols] = _dot((lo + rev).astype(BF16), cos_c).astype(BF16)
            p_ref[half + bs * i:half + bs * (i + 1), cols] = _dot((lo - rev).astype(BF16), sin_c).astype(BF16)
        ah_ref[:, cols] = _dot(u_ref[half:half + 16, :], cos_c)


def _fourier_rows(m_ref, p_ref, ah_ref, o_ref, n):
    rows = m_ref.shape[0]
    parity = lax.broadcasted_iota(jnp.int32, (rows, 1), 0) & 1
    sign = (1 - 2 * parity).astype(F32)
    nyquist = ah_ref[0:1, :] * (n ** -0.5)
    o_ref[...] = (_dot(m_ref[...], p_ref[0:n, :]) + sign * nyquist).astype(BF16)


def _fourier_kernel(ul0, ul1, ul2, ul3, uc0, uc1, uc2, uc3, chan_ref, ml_ref, mc_ref, o_ref, p_ref, ah_ref, *,
                    nm, seq_len, ctx_len):
    mi = pl.program_id(1)

    @pl.when(mi == 0)
    def _():
        _fourier_fold((ul0, ul1, ul2, ul3), chan_ref, p_ref, ah_ref, seq_len)

    @pl.when(mi < nm)
    def _():
        _fourier_rows(ml_ref, p_ref, ah_ref, o_ref, seq_len)

    @pl.when(mi == nm)
    def _():
        _fourier_fold((uc0, uc1, uc2, uc3), chan_ref, p_ref, ah_ref, ctx_len)
        _fourier_rows(mc_ref, p_ref, ah_ref, o_ref, ctx_len)


def _fourier(proj, chan_mat, seq_mat_l, seq_mat_c, *, n_batch, seq_len, ctx_len):
    rows = proj.shape[0]
    tm = TM_FOURIER
    nm = seq_len // tm
    gd = FOURIER_GROUP_DIM
    col0 = COL_U // gd
    ctx_blk0 = n_batch * seq_len // ctx_len

    def u_specs(block_rows, blk0):
        return [pl.BlockSpec((block_rows, gd), functools.partial(lambda b, mi, g: (blk0 + b, col0 + g), g=g))
                for g in range(FOURIER_GROUPS)]

    def out_map(b, mi):
        return (jnp.where(mi < nm, b * nm + mi, n_batch * nm + b), 0)

    return pl.pallas_call(
        functools.partial(_fourier_kernel, nm=nm, seq_len=seq_len, ctx_len=ctx_len),
        grid=(n_batch, nm + 1),
        in_specs=u_specs(seq_len, 0) + u_specs(ctx_len, ctx_blk0) + [
            pl.BlockSpec((gd, 2 * gd), lambda b, mi: (0, 0)),
            pl.BlockSpec((tm, seq_len), lambda b, mi: (jnp.minimum(mi, nm - 1), 0)),
            pl.BlockSpec((ctx_len, ctx_len), lambda b, mi: (0, 0)),
        ],
        out_specs=pl.BlockSpec((tm, FOURIER_DIM), out_map),
        out_shape=jax.ShapeDtypeStruct((rows, FOURIER_DIM), BF16),
        scratch_shapes=[pltpu.VMEM((seq_len, FOURIER_DIM), BF16), pltpu.VMEM((16, FOURIER_DIM), F32)],
        compiler_params=_cparams(("parallel", "arbitrary")),
        name="fourier",
    )(*([proj] * (2 * FOURIER_GROUPS)), chan_mat, seq_mat_l, seq_mat_c)


def _dft_cos_sin(n, n_cols):
    j = jnp.arange(n, dtype=jnp.int32)
    jk = (j[:, None] * j[None, :n_cols]) % n
    ang = jk.astype(F32) * (2.0 * math.pi / n)
    return jnp.cos(ang), jnp.sin(ang)


def _dft_tables(n):
    cos_m, sin_m = _dft_cos_sin(n, n // 2)
    return (jnp.concatenate([cos_m, -sin_m], axis=1) * (n ** -0.5)).astype(BF16)


def _chan_table():
    cos_m, sin_m = _dft_cos_sin(FOURIER_GROUP_DIM, FOURIER_GROUP_DIM)
    return (jnp.concatenate([cos_m, sin_m], axis=1) * (FOURIER_GROUP_DIM ** -0.5)).astype(BF16)


def _merge_kernel(x_ref, mod_ref, yf_ref, yb_ref, xh_ref, z_ref, four_ref, attn_ref, g_ref, wb_ref, wo_ref,
                  dsk_ref, ng_ref, lg_ref, lb_ref, o_ref):
    y = yf_ref[...] + yb_ref[...] + dsk_ref[...] * xh_ref[...].astype(F32)
    y = y * _silu(z_ref[...].astype(F32))
    gw = SSD_DIM // SSD_GROUPS
    heads = []
    for g in range(SSD_GROUPS):
        yg = y[:, g * gw:(g + 1) * gw]
        yg = yg * lax.rsqrt(jnp.mean(yg * yg, axis=-1, keepdims=True) + RMS_EPS)
        heads.append((yg * ng_ref[:, g * gw:(g + 1) * gw]).astype(BF16))
    branches = (jnp.concatenate(heads, axis=1), four_ref[...], attn_ref[...])

    acc = None
    for j in range(D_MODEL // TN_MERGE):
        lo = j * TN_MERGE
        m = None
        for k, yk in enumerate(branches):
            gate = jax.nn.sigmoid(g_ref[:, k * D_MODEL + lo:k * D_MODEL + lo + TN_MERGE].astype(F32))
            term = gate * _dot(yk, wb_ref[k, :, lo:lo + TN_MERGE])
            m = term if m is None else m + term
        part = _dot(m.astype(BF16), wo_ref[lo:lo + TN_MERGE, :])
        acc = part if acc is None else acc + part

    v = ALPHA * x_ref[...] + mod_ref[5:6, :] * acc
    o_ref[...] = _layer_norm(v, lg_ref[...], lb_ref[...])


def _merge(xs, rows, mods, y_ssd, xbc, proj, four, attn, w_branch, w_out, dsk, norm_g, ln_g, ln_b, *,
           layer, seq_len, n_batch):
    tiles_per_batch = seq_len // TM_MERGE
    resident = pl.Buffered(1)
    row_spec = pl.BlockSpec((TM_MERGE, BRANCH_DIM), lambda i: (i, 0))
    vec_spec = pl.BlockSpec((None, 1, BRANCH_DIM), lambda i: (layer, 0, 0))
    ln_spec = pl.BlockSpec((None, None, 1, D_MODEL), lambda i: (layer, 1, 0, 0))
    return pl.pallas_call(
        _merge_kernel,
        grid=(rows // TM_MERGE,),
        in_specs=[
            pl.BlockSpec((TM_MERGE, D_MODEL), lambda i: (i, 0)),
            pl.BlockSpec((None, None, N_MOD, D_MODEL),
                         lambda i: (layer, jnp.minimum(i // tiles_per_batch, n_batch), 0, 0)),
            pl.BlockSpec((None, TM_MERGE, SSD_DIM), lambda i: (0, i, 0)),
            pl.BlockSpec((None, TM_MERGE, SSD_DIM), lambda i: (1, i, 0)),
            row_spec,
            pl.BlockSpec((TM_MERGE, BRANCH_DIM), lambda i: (i, COL_Z // BRANCH_DIM)),
            row_spec,
            row_spec,
            pl.BlockSpec((TM_MERGE, N_BRANCH * D_MODEL), lambda i: (i, COL_G // (N_BRANCH * D_MODEL))),
            pl.BlockSpec((None, N_BRANCH, BRANCH_DIM, D_MODEL), lambda i: (layer, 0, 0, 0), pipeline_mode=resident),
            pl.BlockSpec((None, D_MODEL, D_MODEL), lambda i: (layer, 0, 0), pipeline_mode=resident),
            vec_spec, vec_spec,
            ln_spec, ln_spec,
        ],
        out_specs=pl.BlockSpec((TM_MERGE, D_MODEL), lambda i: (i, 0)),
        out_shape=jax.ShapeDtypeStruct((rows, D_MODEL), F32),
        compiler_params=_cparams(("parallel",)),
        name="merge",
    )(xs, mods, y_ssd, y_ssd, xbc, proj, four, attn, proj, w_branch, w_out, dsk, norm_g, ln_g, ln_b)


def kernel(x, c, ctx, c_ctx, w_ada, b_ada, ln_g, ln_b, ffn_wgu, ffn_wd, w_in, conv_w, conv_b, dt_bias, a_log,
           d_skip, ssd_norm_g, q_norm_g, k_norm_g, w_branch, w_out):
    n_batch, seq_len, _ = x.shape
    ctx_len = ctx.shape[1]
    lat_rows = n_batch * seq_len
    rows = lat_rows + n_batch * ctx_len
    assert ctx_len == TQ and seq_len % TM == 0 and (n_batch * ctx_len) % TM == 0 and n_batch < MOD_ROWS
    tiles_per_batch = seq_len // TM
    tile_kw = dict(tiles_per_batch=tiles_per_batch, n_batch=n_batch)

    nf = D_FF // TF
    wgu_bf = (ffn_wgu.astype(BF16).reshape(DEPTH, 2, D_MODEL, 2, nf, TF)
              .transpose(0, 1, 4, 2, 3, 5).reshape(DEPTH, 2, nf, D_MODEL, 2 * TF))
    wd_bf = ffn_wd.astype(BF16)
    wb_bf = w_branch.astype(BF16)
    wo_bf = w_out.astype(BF16)
    w_z, w_xbc, w_dtf, w_dtb, w_u, w_q, w_k, w_v, w_g = jnp.split(w_in, IN_SPLITS, axis=-1)
    w_main = jnp.concatenate([w_g, w_z, w_u, w_q, w_xbc, w_k, w_v], axis=-1).astype(BF16)
    w_main = w_main.reshape(DEPTH, D_MODEL, PROJ_W // TN_PROJ, TN_PROJ).transpose(0, 2, 1, 3)
    w_dt = jnp.concatenate([w_dtf, w_dtb, jnp.zeros((DEPTH, D_MODEL, DT_W - 2 * SSD_HEADS), F32)],
                           axis=-1).astype(BF16)
    head_pad = ((0, 0), (0, 0), (0, DT_W - SSD_HEADS))
    bias_row = jnp.pad(dt_bias, head_pad).reshape(DEPTH, 2, 1, DT_W)
    alog_row = jnp.pad(a_log, head_pad).reshape(DEPTH, 2, 1, DT_W)
    bias_col = dt_bias.reshape(DEPTH, 2, SSD_HEADS, 1)
    alog_col = a_log.reshape(DEPTH, 2, SSD_HEADS, 1)
    dsk_all = jnp.repeat(d_skip, SSD_HEAD_DIM, axis=-1).reshape(DEPTH, 1, SSD_DIM)
    norm_g_all = ssd_norm_g.reshape(DEPTH, 1, SSD_DIM)
    qk_gains = jnp.stack([q_norm_g, k_norm_g], axis=1)
    ln_g4 = ln_g.reshape(DEPTH, 3, 1, D_MODEL)
    ln_b4 = ln_b.reshape(DEPTH, 3, 1, D_MODEL)
    cos_t, sin_t = _rope_tables(seq_len)
    chan_mat = _chan_table()
    seq_mat_l = _dft_tables(seq_len)
    seq_mat_c = _dft_tables(ctx_len)

    cvec = jnp.concatenate([c, c_ctx[None], jnp.zeros((MOD_ROWS - n_batch - 1, D_MODEL), F32)], axis=0)
    mods = _adaln_all(cvec, w_ada, b_ada)

    xs = jnp.concatenate([x.reshape(lat_rows, D_MODEL), ctx.reshape(n_batch * ctx_len, D_MODEL)], axis=0)
    for layer in range(DEPTH):
        last = layer == DEPTH - 1
        xs = _ffn(xs, rows, mods, wgu_bf, wd_bf, ln_g4, ln_b4, layer=layer, which=0, sub=0, **tile_kw)

        proj, dt = _inproj(xs, mods, w_main, w_dt, layer=layer, **tile_kw)
        xbc = _conv(proj, conv_w, conv_b, layer=layer, lat_rows=lat_rows, seq_len=seq_len)
        y_ssd = _ssd(xbc, dt, dt.T, bias_row, bias_col, alog_row, alog_col,
                     layer=layer, n_batch=n_batch, seq_len=seq_len, ctx_len=ctx_len)
        qk = _qk_prep(proj, qk_gains[layer], cos_t, sin_t, lat_rows=lat_rows, seq_len=seq_len)
        attn = _attention(qk, proj, n_batch=n_batch, seq_len=seq_len, ctx_len=ctx_len)
        four = _fourier(proj, chan_mat, seq_mat_l, seq_mat_c, n_batch=n_batch, seq_len=seq_len, ctx_len=ctx_len)
        out_rows = lat_rows if last else rows
        xs = _merge(xs, out_rows, mods, y_ssd, xbc, proj, four, attn, wb_bf, wo_bf, dsk_all, norm_g_all,
                    ln_g4, ln_b4, layer=layer, seq_len=seq_len, n_batch=n_batch)
        xs = _ffn(xs, out_rows, mods, wgu_bf, wd_bf, ln_g4, ln_b4, layer=layer, which=1, sub=2, **tile_kw)
    return xs.reshape(n_batch, seq_len, D_MODEL)
```

```python
import functools
import math

import jax
import jax.numpy as jnp
from jax import lax
from jax.experimental import pallas as pl
from jax.experimental.pallas import tpu as pltpu

F32 = jnp.float32
BF16 = jnp.bfloat16

D_MODEL = 2048
DEPTH = 4
GRID_W = 64
N_MOD = 9
ALPHA = (2 * DEPTH) ** 0.25
LN_EPS = 1e-6
RMS_EPS = 1e-6
D_FF = 5632
BRANCH_DIM = D_MODEL // 2
N_BRANCH = 3
SSD_DIM = BRANCH_DIM
SSD_HEAD_DIM = 64
SSD_HEADS = SSD_DIM // SSD_HEAD_DIM
SSD_GROUPS = 2
HEADS_PER_GROUP = SSD_HEADS // SSD_GROUPS
SSD_STATE = 128
SSD_CHUNK = 128
D_CONV = 5
CONV_DIM = SSD_DIM + 2 * SSD_GROUPS * SSD_STATE
FOURIER_DIM = BRANCH_DIM
FOURIER_GROUPS = 4
FOURIER_GROUP_DIM = FOURIER_DIM // FOURIER_GROUPS
HEAD_DIM = 128
N_Q_HEADS = BRANCH_DIM // HEAD_DIM
N_KV_HEADS = 2
Q_PER_KV = N_Q_HEADS // N_KV_HEADS
ATTN_DIM = N_Q_HEADS * HEAD_DIM
KV_DIM = N_KV_HEADS * HEAD_DIM
AXIS_ROPE_DIM = HEAD_DIM // 2
ROPE_THETA = 10000.0
IN_WIDTHS = (SSD_DIM, CONV_DIM, SSD_HEADS, SSD_HEADS, FOURIER_DIM, ATTN_DIM, KV_DIM, KV_DIM, N_BRANCH * D_MODEL)
IN_SPLITS = tuple(sum(IN_WIDTHS[:i + 1]) for i in range(len(IN_WIDTHS) - 1))

COL_G = 0
COL_Z = COL_G + N_BRANCH * D_MODEL
COL_U = COL_Z + SSD_DIM
COL_Q = COL_U + FOURIER_DIM
COL_XBC = COL_Q + ATTN_DIM
COL_K = COL_XBC + CONV_DIM
COL_V = COL_K + KV_DIM
PROJ_W = COL_V + KV_DIM
DT_W = 128
MOD_ROWS = 16

TM = 512
TF = 512
TN_PROJ = 2816
FFN_SPLIT = 2
TM_MERGE = 256
TN_MERGE = 512
TN_ADA = 1024
CONV_ROWS = 256
CONV_HALO = 16
TQ = 256
TK = 512
TM_FOURIER = 256
VMEM_LIMIT = 56 * 1024 * 1024


def _cparams(sem):
    return pltpu.CompilerParams(dimension_semantics=sem, vmem_limit_bytes=VMEM_LIMIT)


def _silu(x):
    return x * jax.nn.sigmoid(x)


def _softplus(x):
    return jnp.maximum(x, 0.0) + jnp.log1p(jnp.exp(-jnp.abs(x)))


def _layer_norm(v, g, b):
    mu = jnp.mean(v, axis=-1, keepdims=True)
    d = v - mu
    var = jnp.mean(d * d, axis=-1, keepdims=True)
    return d * lax.rsqrt(var + LN_EPS) * g + b


def _dot(a, b):
    return jnp.dot(a, b, preferred_element_type=F32)


def _dot_nt(a, b):
    return lax.dot_general(a, b, (((1,), (1,)), ((), ())), preferred_element_type=F32)


def _dot_tn(a, b):
    return lax.dot_general(a, b, (((0,), (0,)), ((), ())), preferred_element_type=F32)


def _split2(a):
    hi = a.astype(BF16)
    lo = (a - hi.astype(F32)).astype(BF16)
    return hi, lo


def _split3(a):
    hi = a.astype(BF16)
    r = a - hi.astype(F32)
    mid = r.astype(BF16)
    lo = (r - mid.astype(F32)).astype(BF16)
    return hi, mid, lo


def _ada_kernel(c_ref, w_ref, b_ref, o_ref):
    h = _silu(c_ref[...]).astype(BF16)
    o_ref[...] = _dot(h, w_ref[...].astype(BF16)) + b_ref[...]


def _adaln_all(cvec, w_ada, b_ada):
    n = N_MOD * D_MODEL
    out = pl.pallas_call(
        _ada_kernel,
        grid=(DEPTH, n // TN_ADA),
        in_specs=[
            pl.BlockSpec((MOD_ROWS, D_MODEL), lambda l, j: (0, 0)),
            pl.BlockSpec((None, D_MODEL, TN_ADA), lambda l, j: (l, 0, j)),
            pl.BlockSpec((None, 1, TN_ADA), lambda l, j: (l, 0, j)),
        ],
        out_specs=pl.BlockSpec((None, MOD_ROWS, TN_ADA), lambda l, j: (l, 0, j)),
        out_shape=jax.ShapeDtypeStruct((DEPTH, MOD_ROWS, n), F32),
        compiler_params=_cparams(("parallel", "parallel")),
        name="adaln",
    )(cvec, w_ada, b_ada.reshape(DEPTH, 1, n))
    return out.reshape(DEPTH, MOD_ROWS, N_MOD, D_MODEL)


def _mod_spec(layer, tiles_per_batch, n_batch):
    return pl.BlockSpec((None, None, N_MOD, D_MODEL),
                        lambda i, j: (layer, jnp.minimum(i // tiles_per_batch, n_batch), 0, 0))


def _ln_spec(layer, sub):
    return pl.BlockSpec((None, None, 1, D_MODEL), lambda i, j: (layer, sub, 0, 0))


def _ffn_kernel(x_ref, mod_ref, wg_ref, wu_ref, wd_ref, g_ref, b_ref, o_ref, h_ref, *, sub, nj):
    j = pl.program_id(1)

    @pl.when(j == 0)
    def _():
        shift = mod_ref[3 * sub:3 * sub + 1, :]
        scale = mod_ref[3 * sub + 1:3 * sub + 2, :]
        h_ref[...] = (x_ref[...] * (1.0 + scale) + shift).astype(BF16)
        o_ref[...] = jnp.zeros_like(o_ref)

    rows = h_ref.shape[0] // FFN_SPLIT
    for r in range(FFN_SPLIT):
        h = h_ref[r * rows:(r + 1) * rows, :]
        act = (_silu(_dot(h, wg_ref[...])) * _dot(h, wu_ref[...])).astype(BF16)
        o_ref[r * rows:(r + 1) * rows, :] += _dot(act, wd_ref[...])

    @pl.when(j == nj - 1)
    def _():
        gain = mod_ref[3 * sub + 2:3 * sub + 3, :]
        v = ALPHA * x_ref[...] + 0.5 * gain * o_ref[...]
        o_ref[...] = _layer_norm(v, g_ref[...], b_ref[...])


def _ffn(xs, rows, mods, wgu, wd, ln_g, ln_b, *, layer, which, sub, tiles_per_batch, n_batch):
    nj = D_FF // TF
    return pl.pallas_call(
        functools.partial(_ffn_kernel, sub=sub, nj=nj),
        grid=(rows // TM, nj),
        in_specs=[
            pl.BlockSpec((TM, D_MODEL), lambda i, j: (i, 0)),
            _mod_spec(layer, tiles_per_batch, n_batch),
            pl.BlockSpec((None, None, D_MODEL, TF), lambda i, j: (layer, which, 0, j)),
            pl.BlockSpec((None, None, D_MODEL, TF), lambda i, j: (layer, which, 0, nj + j)),
            pl.BlockSpec((None, None, TF, D_MODEL), lambda i, j: (layer, which, j, 0)),
            _ln_spec(layer, sub),
            _ln_spec(layer, sub),
        ],
        out_specs=pl.BlockSpec((TM, D_MODEL), lambda i, j: (i, 0)),
        out_shape=jax.ShapeDtypeStruct((rows, D_MODEL), F32),
        scratch_shapes=[pltpu.VMEM((TM, D_MODEL), BF16)],
        compiler_params=_cparams(("parallel", "arbitrary")),
        name="ffn",
    )(xs, mods, wgu, wgu, wd, ln_g, ln_b)


def _inproj_kernel(x_ref, mod_ref, w_ref, wdt_ref, o_ref, dt_ref, h_ref):
    j = pl.program_id(1)

    @pl.when(j == 0)
    def _():
        shift = mod_ref[3:4, :]
        scale = mod_ref[4:5, :]
        h = (x_ref[...] * (1.0 + scale) + shift).astype(BF16)
        h_ref[...] = h
        dt_ref[...] = _dot(h, wdt_ref[...])

    o_ref[...] = _dot(h_ref[...], w_ref[...]).astype(BF16)


def _inproj(xs, mods, w_main, w_dt, *, layer, tiles_per_batch, n_batch):
    rows = xs.shape[0]
    return pl.pallas_call(
        _inproj_kernel,
        grid=(rows // TM, PROJ_W // TN_PROJ),
        in_specs=[
            pl.BlockSpec((TM, D_MODEL), lambda i, j: (i, 0)),
            _mod_spec(layer, tiles_per_batch, n_batch),
            pl.BlockSpec((None, D_MODEL, TN_PROJ), lambda i, j: (layer, 0, j)),
            pl.BlockSpec((None, D_MODEL, DT_W), lambda i, j: (layer, 0, 0)),
        ],
        out_specs=[
            pl.BlockSpec((TM, TN_PROJ), lambda i, j: (i, j)),
            pl.BlockSpec((TM, DT_W), lambda i, j: (i, 0)),
        ],
        out_shape=[
            jax.ShapeDtypeStruct((rows, PROJ_W), BF16),
            jax.ShapeDtypeStruct((rows, DT_W), F32),
        ],
        scratch_shapes=[pltpu.VMEM((TM, D_MODEL), BF16)],
        compiler_params=_cparams(("parallel", "arbitrary")),
        name="inproj",
    )(xs, mods, w_main, w_dt)


def _conv_kernel(prev_ref, cur_ref, next_ref, w_ref, b_ref, o_ref, *, lat_tiles, tiles_per_seq):
    k = pl.program_id(0)
    in_ctx = k >= lat_tiles
    pos = k % tiles_per_seq
    first = jnp.logical_or(in_ctx, pos == 0)
    last = jnp.logical_or(in_ctx, pos == tiles_per_seq - 1)
    cur = cur_ref[...].astype(F32)
    n = cur.shape[0]
    prev = jnp.where(first, 0.0, prev_ref[...].astype(F32))
    nxt = jnp.where(last, 0.0, next_ref[...].astype(F32))
    row = lax.broadcasted_iota(jnp.int32, cur.shape, 0)
    w = w_ref[...]
    acc = cur * w[2:3, :] + b_ref[...]
    s = jnp.where(row == 0, prev[CONV_HALO - 1:CONV_HALO, :], pltpu.roll(cur, 1, 0))
    acc += s * w[1:2, :]
    s = jnp.where(row == 0, prev[CONV_HALO - 2:CONV_HALO - 1, :],
                  jnp.where(row == 1, prev[CONV_HALO - 1:CONV_HALO, :], pltpu.roll(cur, 2, 0)))
    acc += s * w[0:1, :]
    s = jnp.where(row == n - 1, nxt[0:1, :], pltpu.roll(cur, n - 1, 0))
    acc += s * w[3:4, :]
    s = jnp.where(row == n - 1, nxt[1:2, :],
                  jnp.where(row == n - 2, nxt[0:1, :], pltpu.roll(cur, n - 2, 0)))
    acc += s * w[4:5, :]
    o_ref[...] = _silu(acc).astype(BF16)


def _conv(proj, conv_w, conv_b, *, layer, lat_rows, seq_len):
    rows = proj.shape[0]
    halo_per_tile = CONV_ROWS // CONV_HALO
    n_halo = rows // CONV_HALO
    col0 = COL_XBC // CONV_DIM
    return pl.pallas_call(
        functools.partial(_conv_kernel, lat_tiles=lat_rows // CONV_ROWS, tiles_per_seq=seq_len // CONV_ROWS),
        grid=(rows // CONV_ROWS,),
        in_specs=[
            pl.BlockSpec((CONV_HALO, CONV_DIM), lambda k: (jnp.maximum(k * halo_per_tile - 1, 0), col0)),
            pl.BlockSpec((CONV_ROWS, CONV_DIM), lambda k: (k, col0)),
            pl.BlockSpec((CONV_HALO, CONV_DIM), lambda k: (jnp.minimum((k + 1) * halo_per_tile, n_halo - 1), col0)),
            pl.BlockSpec((None, D_CONV, CONV_DIM), lambda k: (layer, 0, 0)),
            pl.BlockSpec((None, 1, CONV_DIM), lambda k: (layer, 0, 0)),
        ],
        out_specs=pl.BlockSpec((CONV_ROWS, CONV_DIM), lambda k: (k, 0)),
        out_shape=jax.ShapeDtypeStruct((rows, CONV_DIM), BF16),
        compiler_params=_cparams(("parallel",)),
        name="conv",
    )(proj, proj, proj, conv_w, conv_b.reshape(DEPTH, 1, CONV_DIM))


def _qk_kernel(q_ref, k_ref, g_ref, cos_ref, sin_ref, o_ref, *, lat_tiles):
    is_lat = pl.program_id(0) < lat_tiles
    cos = jnp.where(is_lat, cos_ref[...], 1.0)
    sin = jnp.where(is_lat, sin_ref[...], 0.0)
    lane = lax.broadcasted_iota(jnp.int32, cos.shape, 1)
    half = AXIS_ROPE_DIM // 2
    first_half = (lane % AXIS_ROPE_DIM) < half
    q_scale = HEAD_DIM ** -0.5 * math.log2(math.e)
    for h in range(N_Q_HEADS + N_KV_HEADS):
        if h < N_Q_HEADS:
            x = q_ref[:, h * HEAD_DIM:(h + 1) * HEAD_DIM].astype(F32)
            gain = g_ref[0:1, :] * q_scale
        else:
            x = k_ref[:, (h - N_Q_HEADS) * HEAD_DIM:(h - N_Q_HEADS + 1) * HEAD_DIM].astype(F32)
            gain = g_ref[1:2, :]
        xn = x * lax.rsqrt(jnp.mean(x * x, axis=-1, keepdims=True) + RMS_EPS) * gain
        partner = jnp.where(first_half, pltpu.roll(xn, HEAD_DIM - half, 1), pltpu.roll(xn, half, 1))
        o_ref[:, h * HEAD_DIM:(h + 1) * HEAD_DIM] = (xn * cos + partner * sin).astype(BF16)


def _qk_prep(proj, gains, cos_t, sin_t, *, lat_rows, seq_len):
    rows = proj.shape[0]
    width = ATTN_DIM + KV_DIM
    tiles_per_seq = seq_len // TM
    return pl.pallas_call(
        functools.partial(_qk_kernel, lat_tiles=lat_rows // TM),
        grid=(rows // TM,),
        in_specs=[
            pl.BlockSpec((TM, ATTN_DIM), lambda i: (i, COL_Q // ATTN_DIM)),
            pl.BlockSpec((TM, KV_DIM), lambda i: (i, COL_K // KV_DIM)),
            pl.BlockSpec((2, HEAD_DIM), lambda i: (0, 0)),
            pl.BlockSpec((TM, HEAD_DIM), lambda i: (i % tiles_per_seq, 0)),
            pl.BlockSpec((TM, HEAD_DIM), lambda i: (i % tiles_per_seq, 0)),
        ],
        out_specs=pl.BlockSpec((TM, width), lambda i: (i, 0)),
        out_shape=jax.ShapeDtypeStruct((rows, width), BF16),
        compiler_params=_cparams(("parallel",)),
        name="qk_prep",
    )(proj, proj, gains, cos_t, sin_t)


def _rope_tables(seq_len):
    t = jnp.arange(seq_len)
    row = (t // GRID_W).astype(F32)
    col = (t % GRID_W).astype(F32)
    inv_freq = ROPE_THETA ** (-jnp.arange(0, AXIS_ROPE_DIM, 2, dtype=F32) / AXIS_ROPE_DIM)
    ang_r = row[:, None] * inv_freq
    ang_c = col[:, None] * inv_freq
    cos_t = jnp.concatenate([jnp.cos(ang_r), jnp.cos(ang_r), jnp.cos(ang_c), jnp.cos(ang_c)], axis=-1)
    sin_t = jnp.concatenate([-jnp.sin(ang_r), jnp.sin(ang_r), -jnp.sin(ang_c), jnp.sin(ang_c)], axis=-1)
    return cos_t, sin_t


def _with_ones(v):
    return jnp.concatenate([v, jnp.ones_like(v)], axis=1)


def _attn_kernel(q_ref, kl_ref, kc_ref, vl_ref, vc_ref, o_ref, s_ref, ve_ref, *, nq, ctx_len, seq_len):
    qi = pl.program_id(2)
    chunks = [(0, ctx_len)] + [(ctx_len + c * TK, TK) for c in range(seq_len // TK)]

    @pl.when(qi == 0)
    def _():
        ve_ref[0:ctx_len, :] = _with_ones(vc_ref[...])
        ve_ref[ctx_len:, :] = _with_ones(vl_ref[...])

    def keys(lo, n):
        return kc_ref[...] if lo == 0 else kl_ref[lo - ctx_len:lo - ctx_len + n, :]

    def scores(h, lo, n, m_lanes):
        s = _dot_nt(q_ref[:, h * HEAD_DIM:(h + 1) * HEAD_DIM], keys(lo, n))
        s_ref[h % 2, :, lo:lo + n] = s
        for k in range(n // HEAD_DIM):
            blk = s[:, k * HEAD_DIM:(k + 1) * HEAD_DIM]
            m_lanes = blk if m_lanes is None else jnp.maximum(m_lanes, blk)
        return m_lanes

    def weighted(h, lo, n, m, o_ext):
        p = jnp.exp2(s_ref[h % 2, :, lo:lo + n] - m).astype(BF16)
        part = _dot(p, ve_ref[lo:lo + n, :])
        return part if o_ext is None else o_ext + part

    def finish(h, o_ext):
        o_ref[:, h * HEAD_DIM:(h + 1) * HEAD_DIM] = (o_ext[:, :HEAD_DIM] / o_ext[:, HEAD_DIM:]).astype(BF16)

    def run(chunk_list):
        m_lanes = None
        for lo, n in chunk_list:
            m_lanes = scores(0, lo, n, m_lanes)
        for h in range(Q_PER_KV):
            m = jnp.max(m_lanes, axis=-1, keepdims=True)
            m_lanes, o_ext = None, None
            for lo, n in chunk_list:
                o_ext = weighted(h, lo, n, m, o_ext)
                if h + 1 < Q_PER_KV:
                    m_lanes = scores(h + 1, lo, n, m_lanes)
            finish(h, o_ext)

    @pl.when(qi < nq)
    def _():
        run(chunks)

    @pl.when(qi == nq)
    def _():
        run(chunks[:1])


def _attention(qk, proj, *, n_batch, seq_len, ctx_len):
    rows = qk.shape[0]
    nq = seq_len // TQ
    ctx_blk0 = n_batch * seq_len // ctx_len
    kcol = N_Q_HEADS
    vcol = COL_V // HEAD_DIM
    qw = Q_PER_KV * HEAD_DIM

    def q_map(b, g, qi):
        return (jnp.where(qi < nq, b * nq + qi, n_batch * nq + b), g)

    return pl.pallas_call(
        functools.partial(_attn_kernel, nq=nq, ctx_len=ctx_len, seq_len=seq_len),
        grid=(n_batch, N_KV_HEADS, nq + 1),
        in_specs=[
            pl.BlockSpec((TQ, qw), q_map),
            pl.BlockSpec((seq_len, HEAD_DIM), lambda b, g, qi: (b, kcol + g)),
            pl.BlockSpec((ctx_len, HEAD_DIM), lambda b, g, qi: (ctx_blk0 + b, kcol + g)),
            pl.BlockSpec((seq_len, HEAD_DIM), lambda b, g, qi: (b, vcol + g)),
            pl.BlockSpec((ctx_len, HEAD_DIM), lambda b, g, qi: (ctx_blk0 + b, vcol + g)),
        ],
        out_specs=pl.BlockSpec((TQ, qw), q_map),
        out_shape=jax.ShapeDtypeStruct((rows, ATTN_DIM), BF16),
        scratch_shapes=[pltpu.VMEM((2, TQ, ctx_len + seq_len), F32),
                        pltpu.VMEM((ctx_len + seq_len, 2 * HEAD_DIM), BF16)],
        compiler_params=_cparams(("parallel", "parallel", "arbitrary")),
        name="attention",
    )(qk, qk, qk, proj, proj)


def _ssd_expand_table():
    er = jnp.arange(2 * DT_W)[:, None] % DT_W
    ec = jnp.arange(3 * SSD_DIM)[None, :]
    return (er == (ec // SSD_DIM) * SSD_HEADS + (ec % SSD_DIM) // SSD_HEAD_DIM).astype(BF16)


def _ssd_chunk(fwd, x_ref, b_ref, c_ref, dt_ref, dtt_ref, bias_ref, biast_ref, alog_ref, alogt_ref,
               expand_ref, y_ref, st_ref):
    nh = SSD_HEADS
    gw = HEADS_PER_GROUP * SSD_HEAD_DIM
    n = SSD_CHUNK

    row = lax.broadcasted_iota(jnp.int32, (n, n), 0)
    col = lax.broadcasted_iota(jnp.int32, (n, n), 1)
    mask = (col <= row) if fwd else (col >= row)
    tri = mask.astype(BF16)
    tri_t = ((row <= col) if fwd else (row >= col)).astype(BF16)

    dt_raw = dt_ref[...] if fwd else pltpu.roll(dt_ref[...], DT_W - nh, 1)
    dtv = _softplus(dt_raw + bias_ref[...])
    da = dtv * (-jnp.exp(alog_ref[...]))
    da3 = _split3(da)
    cs = _dot(tri, da3[0]) + _dot(tri, da3[1]) + _dot(tri, da3[2])
    total = cs[n - 1:n, :] if fwd else cs[0:1, :]

    dtv_t = _softplus(dtt_ref[...] + biast_ref[...])
    da_t = dtv_t * (-jnp.exp(alogt_ref[...]))
    da_t3 = _split3(da_t)
    cs_t = _dot(da_t3[0], tri_t) + _dot(da_t3[1], tri_t) + _dot(da_t3[2], tri_t)

    decay_end = jnp.exp(total - cs)
    ecs = jnp.exp(cs)
    lane = lax.broadcasted_iota(jnp.int32, (n, DT_W), 1)
    packed = jnp.where(lane < nh, dtv,
                       jnp.where(lane < 2 * nh, pltpu.roll(dtv * decay_end, nh, 1),
                                 jnp.where(lane < 3 * nh, pltpu.roll(ecs, 2 * nh, 1), 0.0)))
    ex = _dot(jnp.concatenate(_split2(packed), axis=1), expand_ref[...])
    dt_x = ex[:, 0:SSD_DIM]
    dtdec_x = ex[:, SSD_DIM:2 * SSD_DIM]
    ecs_x = ex[:, 2 * SSD_DIM:3 * SSD_DIM]

    xf = x_ref[...].astype(F32)
    xdt = (xf * dt_x).astype(BF16)
    xdec = (xf * dtdec_x).astype(BF16)
    lane_h = lax.broadcasted_iota(jnp.int32, (n, 2 * SSD_HEAD_DIM), 1)

    for g in range(SSD_GROUPS):
        bg = b_ref[:, g * SSD_STATE:(g + 1) * SSD_STATE]
        cg = c_ref[:, g * SSD_STATE:(g + 1) * SSD_STATE]
        cb = _dot_nt(cg, bg)
        st = st_ref[g]
        y_off = _dot(cg, st.astype(BF16)) * ecs_x[:, g * gw:(g + 1) * gw]
        for j in range(HEADS_PER_GROUP // 2):
            ws = []
            for e in (2 * j, 2 * j + 1):
                hcol = g * HEADS_PER_GROUP + e
                diff = cs[:, hcol:hcol + 1] - cs_t[hcol:hcol + 1, :]
                lm = jnp.exp(jnp.where(mask, diff, -1e30))
                ws.append((cb * lm).astype(BF16))
            w_pair = jnp.concatenate(ws, axis=1)
            lo = g * gw + j * 2 * SSD_HEAD_DIM
            x2 = xdt[:, lo:lo + 2 * SSD_HEAD_DIM]
            rhs = jnp.concatenate([jnp.where(lane_h < SSD_HEAD_DIM, x2, jnp.zeros_like(x2)),
                                   jnp.where(lane_h >= SSD_HEAD_DIM, x2, jnp.zeros_like(x2))], axis=0)
            y_ref[:, lo:lo + 2 * SSD_HEAD_DIM] = (
                _dot(w_pair, rhs) + y_off[:, j * 2 * SSD_HEAD_DIM:(j + 1) * 2 * SSD_HEAD_DIM])
        new_states = _dot_tn(bg, xdec[:, g * gw:(g + 1) * gw])
        etot = ecs_x[n - 1:n, g * gw:(g + 1) * gw] if fwd else ecs_x[0:1, g * gw:(g + 1) * gw]
        st_ref[g] = st * etot + new_states


def _ssd_kernel(*refs):
    n_in = 9
    expand_ref = refs[2 * n_in]
    st_ref = refs[-1]

    @pl.when(pl.program_id(1) == 0)
    def _():
        st_ref[...] = jnp.zeros_like(st_ref)

    for d in range(2):
        _ssd_chunk(d == 0, *refs[d * n_in:(d + 1) * n_in], expand_ref, refs[2 * n_in + 1 + d], st_ref.at[d])


def _ssd(xbc, dt, dt_t, bias, bias_t, alog, alog_t, *, layer, n_batch, seq_len, ctx_len):
    rows = xbc.shape[0]
    ncc = ctx_len // SSD_CHUNK
    ncl = seq_len // SSD_CHUNK
    lat_blk = seq_len // SSD_CHUNK
    ctx_blk0 = n_batch * seq_len // SSD_CHUNK
    bcol = SSD_DIM // (SSD_GROUPS * SSD_STATE)

    def rb(d):
        def block(b, c):
            cc = c if d == 0 else ncc - 1 - c
            lc = c - ncc if d == 0 else ncl - 1 - (c - ncc)
            return jnp.where(c < ncc, ctx_blk0 + b * ncc + cc, b * lat_blk + lc)
        return block

    def operand_specs(d):
        blk = rb(d)
        return [
            pl.BlockSpec((SSD_CHUNK, SSD_DIM), lambda b, c: (blk(b, c), 0)),
            pl.BlockSpec((SSD_CHUNK, SSD_GROUPS * SSD_STATE), lambda b, c: (blk(b, c), bcol)),
            pl.BlockSpec((SSD_CHUNK, SSD_GROUPS * SSD_STATE), lambda b, c: (blk(b, c), bcol + 1)),
            pl.BlockSpec((SSD_CHUNK, DT_W), lambda b, c: (blk(b, c), 0)),
            pl.BlockSpec((SSD_HEADS, SSD_CHUNK), lambda b, c: (d, blk(b, c))),
            pl.BlockSpec((None, None, 1, DT_W), lambda b, c: (layer, d, 0, 0)),
            pl.BlockSpec((None, None, SSD_HEADS, 1), lambda b, c: (layer, d, 0, 0)),
            pl.BlockSpec((None, None, 1, DT_W), lambda b, c: (layer, d, 0, 0)),
            pl.BlockSpec((None, None, SSD_HEADS, 1), lambda b, c: (layer, d, 0, 0)),
        ]

    operands = (xbc, xbc, xbc, dt, dt_t, bias, bias_t, alog, alog_t)
    y_shape = jax.ShapeDtypeStruct((rows, SSD_DIM), F32)
    return pl.pallas_call(
        _ssd_kernel,
        grid=(n_batch, ncc + ncl),
        in_specs=operand_specs(0) + operand_specs(1) + [
            pl.BlockSpec((2 * DT_W, 3 * SSD_DIM), lambda b, c: (0, 0))],
        out_specs=[pl.BlockSpec((SSD_CHUNK, SSD_DIM), lambda b, c: (rb(0)(b, c), 0)),
                   pl.BlockSpec((SSD_CHUNK, SSD_DIM), lambda b, c: (rb(1)(b, c), 0))],
        out_shape=[y_shape, y_shape],
        scratch_shapes=[pltpu.VMEM((2, SSD_GROUPS, SSD_STATE, HEADS_PER_GROUP * SSD_HEAD_DIM), F32)],
        compiler_params=_cparams(("parallel", "arbitrary")),
        name="ssd",
    )(*operands, *operands, _ssd_expand_table())


def _fourier_fold(u_refs, chan_ref, p_ref, ah_ref, n):
    gd = FOURIER_GROUP_DIM
    half = n // 2
    bs = min(256, half)
    nb = half // bs
    r_i = lax.broadcasted_iota(jnp.int32, (bs, bs), 0)
    c_i = lax.broadcasted_iota(jnp.int32, (bs, bs), 1)
    flip = (r_i + c_i == bs).astype(BF16)
    row0 = lax.broadcasted_iota(jnp.int32, (bs, gd), 0) == 0
    cos_c = chan_ref[:, :gd]
    sin_c = chan_ref[:, gd:]
    for g, u_ref in enumerate(u_refs):
        cols = slice(g * gd, (g + 1) * gd)
        for i in range(nb):
            lo = u_ref[bs * i:bs * (i + 1), :].astype(F32)
            rev = _dot(flip, u_ref[bs * (2 * nb - 1 - i):bs * (2 * nb - i), :])
            if i > 0:
                first = u_ref[bs * (2 * nb - i):bs * (2 * nb - i) + 16, :][0:1, :].astype(F32)
                rev = jnp.where(row0, first, rev)
            p_ref[bs * i:bs * (i + 1), cols] = _dot((lo + rev).astype(BF16), cos_c).astype(BF16)
            p_ref[half + bs * i:half + bs * (i + 1), cols] = _dot((lo - rev).astype(BF16), sin_c).astype(BF16)
        ah_ref[:, cols] = _dot(u_ref[half:half + 16, :], cos_c)


def _fourier_rows(m_ref, p_ref, ah_ref, o_ref, n):
    rows = m_ref.shape[0]
    parity = lax.broadcasted_iota(jnp.int32, (rows, 1), 0) & 1
    sign = (1 - 2 * parity).astype(F32)
    nyquist = ah_ref[0:1, :] * (n ** -0.5)
    o_ref[...] = (_dot(m_ref[...], p_ref[0:n, :]) + sign * nyquist).astype(BF16)


def _fourier_kernel(ul0, ul1, ul2, ul3, uc0, uc1, uc2, uc3, chan_ref, ml_ref, mc_ref, o_ref, p_ref, ah_ref, *,
                    nm, seq_len, ctx_len):
    mi = pl.program_id(1)

    @pl.when(mi == 0)
    def _():
        _fourier_fold((ul0, ul1, ul2, ul3), chan_ref, p_ref, ah_ref, seq_len)

    @pl.when(mi < nm)
    def _():
        _fourier_rows(ml_ref, p_ref, ah_ref, o_ref, seq_len)

    @pl.when(mi == nm)
    def _():
        _fourier_fold((uc0, uc1, uc2, uc3), chan_ref, p_ref, ah_ref, ctx_len)
        _fourier_rows(mc_ref, p_ref, ah_ref, o_ref, ctx_len)


def _fourier(proj, chan_mat, seq_mat_l, seq_mat_c, *, n_batch, seq_len, ctx_len):
    rows = proj.shape[0]
    tm = TM_FOURIER
    nm = seq_len // tm
    gd = FOURIER_GROUP_DIM
    col0 = COL_U // gd
    ctx_blk0 = n_batch * seq_len // ctx_len

    def u_specs(block_rows, blk0):
        return [pl.BlockSpec((block_rows, gd), functools.partial(lambda b, mi, g: (blk0 + b, col0 + g), g=g))
                for g in range(FOURIER_GROUPS)]

    def out_map(b, mi):
        return (jnp.where(mi < nm, b * nm + mi, n_batch * nm + b), 0)

    return pl.pallas_call(
        functools.partial(_fourier_kernel, nm=nm, seq_len=seq_len, ctx_len=ctx_len),
        grid=(n_batch, nm + 1),
        in_specs=u_specs(seq_len, 0) + u_specs(ctx_len, ctx_blk0) + [
            pl.BlockSpec((gd, 2 * gd), lambda b, mi: (0, 0)),
            pl.BlockSpec((tm, seq_len), lambda b, mi: (jnp.minimum(mi, nm - 1), 0)),
            pl.BlockSpec((ctx_len, ctx_len), lambda b, mi: (0, 0)),
        ],
        out_specs=pl.BlockSpec((tm, FOURIER_DIM), out_map),
        out_shape=jax.ShapeDtypeStruct((rows, FOURIER_DIM), BF16),
        scratch_shapes=[pltpu.VMEM((seq_len, FOURIER_DIM), BF16), pltpu.VMEM((16, FOURIER_DIM), F32)],
        compiler_params=_cparams(("parallel", "arbitrary")),
        name="fourier",
    )(*([proj] * (2 * FOURIER_GROUPS)), chan_mat, seq_mat_l, seq_mat_c)


def _dft_cos_sin(n, n_cols):
    j = jnp.arange(n, dtype=jnp.int32)
    jk = (j[:, None] * j[None, :n_cols]) % n
    ang = jk.astype(F32) * (2.0 * math.pi / n)
    return jnp.cos(ang), jnp.sin(ang)


def _dft_tables(n):
    cos_m, sin_m = _dft_cos_sin(n, n // 2)
    return (jnp.concatenate([cos_m, -sin_m], axis=1) * (n ** -0.5)).astype(BF16)


def _chan_table():
    cos_m, sin_m = _dft_cos_sin(FOURIER_GROUP_DIM, FOURIER_GROUP_DIM)
    return (jnp.concatenate([cos_m, sin_m], axis=1) * (FOURIER_GROUP_DIM ** -0.5)).astype(BF16)


def _merge_kernel(x_ref, mod_ref, yf_ref, yb_ref, xh_ref, z_ref, four_ref, attn_ref, g_ref, wb_ref, wo_ref,
                  dsk_ref, ng_ref, lg_ref, lb_ref, o_ref):
    y = yf_ref[...] + yb_ref[...] + dsk_ref[...] * xh_ref[...].astype(F32)
    y = y * _silu(z_ref[...].astype(F32))
    gw = SSD_DIM // SSD_GROUPS
    heads = []
    for g in range(SSD_GROUPS):
        yg = y[:, g * gw:(g + 1) * gw]
        yg = yg * lax.rsqrt(jnp.mean(yg * yg, axis=-1, keepdims=True) + RMS_EPS)
        heads.append((yg * ng_ref[:, g * gw:(g + 1) * gw]).astype(BF16))
    branches = (jnp.concatenate(heads, axis=1), four_ref[...], attn_ref[...])

    acc = None
    for j in range(D_MODEL // TN_MERGE):
        lo = j * TN_MERGE
        m = None
        for k, yk in enumerate(branches):
            gate = jax.nn.sigmoid(g_ref[:, k * D_MODEL + lo:k * D_MODEL + lo + TN_MERGE].astype(F32))
            term = gate * _dot(yk, wb_ref[k, :, lo:lo + TN_MERGE])
            m = term if m is None else m + term
        part = _dot(m.astype(BF16), wo_ref[lo:lo + TN_MERGE, :])
        acc = part if acc is None else acc + part

    v = ALPHA * x_ref[...] + mod_ref[5:6, :] * acc
    o_ref[...] = _layer_norm(v, lg_ref[...], lb_ref[...])


def _merge(xs, rows, mods, y_ssd, xbc, proj, four, attn, w_branch, w_out, dsk, norm_g, ln_g, ln_b, *,
           layer, seq_len, n_batch):
    tiles_per_batch = seq_len // TM_MERGE
    resident = pl.Buffered(1)
    row_spec = pl.BlockSpec((TM_MERGE, BRANCH_DIM), lambda i: (i, 0))
    vec_spec = pl.BlockSpec((None, 1, BRANCH_DIM), lambda i: (layer, 0, 0))
    ln_spec = pl.BlockSpec((None, None, 1, D_MODEL), lambda i: (layer, 1, 0, 0))
    return pl.pallas_call(
        _merge_kernel,
        grid=(rows // TM_MERGE,),
        in_specs=[
            pl.BlockSpec((TM_MERGE, D_MODEL), lambda i: (i, 0)),
            pl.BlockSpec((None, None, N_MOD, D_MODEL),
                         lambda i: (layer, jnp.minimum(i // tiles_per_batch, n_batch), 0, 0)),
            row_spec,
            row_spec,
            row_spec,
            pl.BlockSpec((TM_MERGE, BRANCH_DIM), lambda i: (i, COL_Z // BRANCH_DIM)),
            row_spec,
            row_spec,
            pl.BlockSpec((TM_MERGE, N_BRANCH * D_MODEL), lambda i: (i, COL_G // (N_BRANCH * D_MODEL))),
            pl.BlockSpec((None, N_BRANCH, BRANCH_DIM, D_MODEL), lambda i: (layer, 0, 0, 0), pipeline_mode=resident),
            pl.BlockSpec((None, D_MODEL, D_MODEL), lambda i: (layer, 0, 0), pipeline_mode=resident),
            vec_spec, vec_spec,
            ln_spec, ln_spec,
        ],
        out_specs=pl.BlockSpec((TM_MERGE, D_MODEL), lambda i: (i, 0)),
        out_shape=jax.ShapeDtypeStruct((rows, D_MODEL), F32),
        compiler_params=_cparams(("parallel",)),
        name="merge",
    )(xs, mods, y_ssd[0], y_ssd[1], xbc, proj, four, attn, proj, w_branch, w_out, dsk, norm_g, ln_g, ln_b)


def kernel(x, c, ctx, c_ctx, w_ada, b_ada, ln_g, ln_b, ffn_wgu, ffn_wd, w_in, conv_w, conv_b, dt_bias, a_log,
           d_skip, ssd_norm_g, q_norm_g, k_norm_g, w_branch, w_out):
    n_batch, seq_len, _ = x.shape
    ctx_len = ctx.shape[1]
    lat_rows = n_batch * seq_len
    rows = lat_rows + n_batch * ctx_len
    assert ctx_len == TQ and seq_len % TM == 0 and (n_batch * ctx_len) % TM == 0 and n_batch < MOD_ROWS
    tiles_per_batch = seq_len // TM
    tile_kw = dict(tiles_per_batch=tiles_per_batch, n_batch=n_batch)

    wgu_bf = ffn_wgu.astype(BF16)
    wd_bf = ffn_wd.astype(BF16)
    wb_bf = w_branch.astype(BF16)
    wo_bf = w_out.astype(BF16)
    w_z, w_xbc, w_dtf, w_dtb, w_u, w_q, w_k, w_v, w_g = jnp.split(w_in, IN_SPLITS, axis=-1)
    w_main = jnp.concatenate([w_g, w_z, w_u, w_q, w_xbc, w_k, w_v], axis=-1).astype(BF16)
    w_dt = jnp.concatenate([w_dtf, w_dtb, jnp.zeros((DEPTH, D_MODEL, DT_W - 2 * SSD_HEADS), F32)],
                           axis=-1).astype(BF16)
    head_pad = ((0, 0), (0, 0), (0, DT_W - SSD_HEADS))
    bias_row = jnp.pad(dt_bias, head_pad).reshape(DEPTH, 2, 1, DT_W)
    alog_row = jnp.pad(a_log, head_pad).reshape(DEPTH, 2, 1, DT_W)
    bias_col = dt_bias.reshape(DEPTH, 2, SSD_HEADS, 1)
    alog_col = a_log.reshape(DEPTH, 2, SSD_HEADS, 1)
    dsk_all = jnp.repeat(d_skip, SSD_HEAD_DIM, axis=-1).reshape(DEPTH, 1, SSD_DIM)
    norm_g_all = ssd_norm_g.reshape(DEPTH, 1, SSD_DIM)
    qk_gains = jnp.stack([q_norm_g, k_norm_g], axis=1)
    ln_g4 = ln_g.reshape(DEPTH, 3, 1, D_MODEL)
    ln_b4 = ln_b.reshape(DEPTH, 3, 1, D_MODEL)
    cos_t, sin_t = _rope_tables(seq_len)
    chan_mat = _chan_table()
    seq_mat_l = _dft_tables(seq_len)
    seq_mat_c = _dft_tables(ctx_len)

    cvec = jnp.concatenate([c, c_ctx[None], jnp.zeros((MOD_ROWS - n_batch - 1, D_MODEL), F32)], axis=0)
    mods = _adaln_all(cvec, w_ada, b_ada)

    xs = jnp.concatenate([x.reshape(lat_rows, D_MODEL), ctx.reshape(n_batch * ctx_len, D_MODEL)], axis=0)
    for layer in range(DEPTH):
        last = layer == DEPTH - 1
        xs = _ffn(xs, rows, mods, wgu_bf, wd_bf, ln_g4, ln_b4, layer=layer, which=0, sub=0, **tile_kw)

        proj, dt = _inproj(xs, mods, w_main, w_dt, layer=layer, **tile_kw)
        xbc = _conv(proj, conv_w, conv_b, layer=layer, lat_rows=lat_rows, seq_len=seq_len)
        y_ssd = _ssd(xbc, dt, dt.T, bias_row, bias_col, alog_row, alog_col,
                     layer=layer, n_batch=n_batch, seq_len=seq_len, ctx_len=ctx_len)
        qk = _qk_prep(proj, qk_gains[layer], cos_t, sin_t, lat_rows=lat_rows, seq_len=seq_len)
        attn = _attention(qk, proj, n_batch=n_batch, seq_len=seq_len, ctx_len=ctx_len)
        four = _fourier(proj, chan_mat, seq_mat_l, seq_mat_c, n_batch=n_batch, seq_len=seq_len, ctx_len=ctx_len)
        out_rows = lat_rows if last else rows
        xs = _merge(xs, out_rows, mods, y_ssd, xbc, proj, four, attn, wb_bf, wo_bf, dsk_all, norm_g_all,
                    ln_g4, ln_b4, layer=layer, seq_len=seq_len, n_batch=n_batch)
        xs = _ffn(xs, out_rows, mods, wgu_bf, wd_bf, ln_g4, ln_b4, layer=layer, which=1, sub=2, **tile_kw)
    return xs.reshape(n_batch, seq_len, D_MODEL)
```

```python
import functools
import math

import jax
import jax.numpy as jnp
from jax import lax
from jax.experimental import pallas as pl
from jax.experimental.pallas import tpu as pltpu

F32 = jnp.float32
BF16 = jnp.bfloat16

D_MODEL = 2048
DEPTH = 4
GRID_W = 64
N_MOD = 9
ALPHA = (2 * DEPTH) ** 0.25
LN_EPS = 1e-6
RMS_EPS = 1e-6
D_FF = 5632
BRANCH_DIM = D_MODEL // 2
N_BRANCH = 3
SSD_DIM = BRANCH_DIM
SSD_HEAD_DIM = 64
SSD_HEADS = SSD_DIM // SSD_HEAD_DIM
SSD_GROUPS = 2
HEADS_PER_GROUP = SSD_HEADS // SSD_GROUPS
SSD_STATE = 128
SSD_CHUNK = 128
D_CONV = 5
CONV_DIM = SSD_DIM + 2 * SSD_GROUPS * SSD_STATE
FOURIER_DIM = BRANCH_DIM
FOURIER_GROUPS = 4
FOURIER_GROUP_DIM = FOURIER_DIM // FOURIER_GROUPS
HEAD_DIM = 128
N_Q_HEADS = BRANCH_DIM // HEAD_DIM
N_KV_HEADS = 2
Q_PER_KV = N_Q_HEADS // N_KV_HEADS
ATTN_DIM = N_Q_HEADS * HEAD_DIM
KV_DIM = N_KV_HEADS * HEAD_DIM
AXIS_ROPE_DIM = HEAD_DIM // 2
ROPE_THETA = 10000.0
IN_WIDTHS = (SSD_DIM, CONV_DIM, SSD_HEADS, SSD_HEADS, FOURIER_DIM, ATTN_DIM, KV_DIM, KV_DIM, N_BRANCH * D_MODEL)
IN_SPLITS = tuple(sum(IN_WIDTHS[:i + 1]) for i in range(len(IN_WIDTHS) - 1))

COL_G = 0
COL_Z = COL_G + N_BRANCH * D_MODEL
COL_U = COL_Z + SSD_DIM
COL_Q = COL_U + FOURIER_DIM
COL_XBC = COL_Q + ATTN_DIM
COL_K = COL_XBC + CONV_DIM
COL_V = COL_K + KV_DIM
PROJ_W = COL_V + KV_DIM
DT_W = 128
MOD_ROWS = 16

TM = 512
TF = 512
TN_PROJ = 2816
FFN_SPLIT = 2
TM_MERGE = 256
TN_MERGE = 512
TN_ADA = 1024
CONV_ROWS = 256
CONV_HALO = 16
TQ = 256
TK = 512
TM_FOURIER = 256
VMEM_LIMIT = 56 * 1024 * 1024


def _cparams(sem):
    return pltpu.CompilerParams(dimension_semantics=sem, vmem_limit_bytes=VMEM_LIMIT)


def _silu(x):
    return x * jax.nn.sigmoid(x)


def _softplus(x):
    return jnp.maximum(x, 0.0) + jnp.log1p(jnp.exp(-jnp.abs(x)))


def _layer_norm(v, g, b):
    mu = jnp.mean(v, axis=-1, keepdims=True)
    d = v - mu
    var = jnp.mean(d * d, axis=-1, keepdims=True)
    return d * lax.rsqrt(var + LN_EPS) * g + b


def _dot(a, b):
    return jnp.dot(a, b, preferred_element_type=F32)


def _dot_nt(a, b):
    return lax.dot_general(a, b, (((1,), (1,)), ((), ())), preferred_element_type=F32)


def _dot_tn(a, b):
    return lax.dot_general(a, b, (((0,), (0,)), ((), ())), preferred_element_type=F32)


def _split2(a):
    hi = a.astype(BF16)
    lo = (a - hi.astype(F32)).astype(BF16)
    return hi, lo


def _split3(a):
    hi = a.astype(BF16)
    r = a - hi.astype(F32)
    mid = r.astype(BF16)
    lo = (r - mid.astype(F32)).astype(BF16)
    return hi, mid, lo


def _ada_kernel(c_ref, w_ref, b_ref, o_ref):
    h = _silu(c_ref[...]).astype(BF16)
    o_ref[...] = _dot(h, w_ref[...].astype(BF16)) + b_ref[...]


def _adaln_all(cvec, w_ada, b_ada):
    n = N_MOD * D_MODEL
    out = pl.pallas_call(
        _ada_kernel,
        grid=(DEPTH, n // TN_ADA),
        in_specs=[
            pl.BlockSpec((MOD_ROWS, D_MODEL), lambda l, j: (0, 0)),
            pl.BlockSpec((None, D_MODEL, TN_ADA), lambda l, j: (l, 0, j)),
            pl.BlockSpec((None, 1, TN_ADA), lambda l, j: (l, 0, j)),
        ],
        out_specs=pl.BlockSpec((None, MOD_ROWS, TN_ADA), lambda l, j: (l, 0, j)),
        out_shape=jax.ShapeDtypeStruct((DEPTH, MOD_ROWS, n), F32),
        compiler_params=_cparams(("parallel", "parallel")),
        name="adaln",
    )(cvec, w_ada, b_ada.reshape(DEPTH, 1, n))
    return out.reshape(DEPTH, MOD_ROWS, N_MOD, D_MODEL)


def _mod_spec(layer, tiles_per_batch, n_batch):
    return pl.BlockSpec((None, None, N_MOD, D_MODEL),
                        lambda i, j: (layer, jnp.minimum(i // tiles_per_batch, n_batch), 0, 0))


def _ln_spec(layer, sub):
    return pl.BlockSpec((None, None, 1, D_MODEL), lambda i, j: (layer, sub, 0, 0))


def _ffn_kernel(x_ref, mod_ref, wg_ref, wu_ref, wd_ref, g_ref, b_ref, o_ref, h_ref, *, sub, nj):
    j = pl.program_id(1)
    rows = h_ref.shape[0] // FFN_SPLIT

    def step(first, last):
        for r in range(FFN_SPLIT):
            rs = slice(r * rows, (r + 1) * rows)
            if first:
                shift = mod_ref[3 * sub:3 * sub + 1, :]
                scale = mod_ref[3 * sub + 1:3 * sub + 2, :]
                h = (x_ref[rs, :] * (1.0 + scale) + shift).astype(BF16)
                h_ref[rs, :] = h
            else:
                h = h_ref[rs, :]
            act = (_silu(_dot(h, wg_ref[...])) * _dot(h, wu_ref[...])).astype(BF16)
            acc = _dot(act, wd_ref[...])
            if not first:
                acc = o_ref[rs, :] + acc
            if last:
                gain = mod_ref[3 * sub + 2:3 * sub + 3, :]
                acc = _layer_norm(ALPHA * x_ref[rs, :] + 0.5 * gain * acc, g_ref[...], b_ref[...])
            o_ref[rs, :] = acc

    @pl.when(j == 0)
    def _():
        step(True, False)

    @pl.when(jnp.logical_and(j > 0, j < nj - 1))
    def _():
        step(False, False)

    @pl.when(j == nj - 1)
    def _():
        step(False, True)


def _ffn(xs, rows, mods, wgu, wd, ln_g, ln_b, *, layer, which, sub, tiles_per_batch, n_batch):
    nj = D_FF // TF
    return pl.pallas_call(
        functools.partial(_ffn_kernel, sub=sub, nj=nj),
        grid=(rows // TM, nj),
        in_specs=[
            pl.BlockSpec((TM, D_MODEL), lambda i, j: (i, 0)),
            _mod_spec(layer, tiles_per_batch, n_batch),
            pl.BlockSpec((None, None, D_MODEL, TF), lambda i, j: (layer, which, 0, j)),
            pl.BlockSpec((None, None, D_MODEL, TF), lambda i, j: (layer, which, 0, nj + j)),
            pl.BlockSpec((None, None, TF, D_MODEL), lambda i, j: (layer, which, j, 0)),
            _ln_spec(layer, sub),
            _ln_spec(layer, sub),
        ],
        out_specs=pl.BlockSpec((TM, D_MODEL), lambda i, j: (i, 0)),
        out_shape=jax.ShapeDtypeStruct((rows, D_MODEL), F32),
        scratch_shapes=[pltpu.VMEM((TM, D_MODEL), BF16)],
        compiler_params=_cparams(("parallel", "arbitrary")),
        name="ffn",
    )(xs, mods, wgu, wgu, wd, ln_g, ln_b)


def _inproj_kernel(x_ref, mod_ref, w_ref, wdt_ref, o_ref, dt_ref, h_ref):
    j = pl.program_id(1)

    @pl.when(j == 0)
    def _():
        shift = mod_ref[3:4, :]
        scale = mod_ref[4:5, :]
        h = (x_ref[...] * (1.0 + scale) + shift).astype(BF16)
        h_ref[...] = h
        dt_ref[...] = _dot(h, wdt_ref[...])

    o_ref[...] = _dot(h_ref[...], w_ref[...]).astype(BF16)


def _inproj(xs, mods, w_main, w_dt, *, layer, tiles_per_batch, n_batch):
    rows = xs.shape[0]
    return pl.pallas_call(
        _inproj_kernel,
        grid=(rows // TM, PROJ_W // TN_PROJ),
        in_specs=[
            pl.BlockSpec((TM, D_MODEL), lambda i, j: (i, 0)),
            _mod_spec(layer, tiles_per_batch, n_batch),
            pl.BlockSpec((None, D_MODEL, TN_PROJ), lambda i, j: (layer, 0, j)),
            pl.BlockSpec((None, D_MODEL, DT_W), lambda i, j: (layer, 0, 0)),
        ],
        out_specs=[
            pl.BlockSpec((TM, TN_PROJ), lambda i, j: (i, j)),
            pl.BlockSpec((TM, DT_W), lambda i, j: (i, 0)),
        ],
        out_shape=[
            jax.ShapeDtypeStruct((rows, PROJ_W), BF16),
            jax.ShapeDtypeStruct((rows, DT_W), F32),
        ],
        scratch_shapes=[pltpu.VMEM((TM, D_MODEL), BF16)],
        compiler_params=_cparams(("parallel", "arbitrary")),
        name="inproj",
    )(xs, mods, w_main, w_dt)


def _conv_kernel(prev_ref, cur_ref, next_ref, w_ref, b_ref, o_ref, *, lat_tiles, tiles_per_seq):
    k = pl.program_id(0)
    in_ctx = k >= lat_tiles
    pos = k % tiles_per_seq
    first = jnp.logical_or(in_ctx, pos == 0)
    last = jnp.logical_or(in_ctx, pos == tiles_per_seq - 1)
    cur = cur_ref[...].astype(F32)
    n = cur.shape[0]
    prev = jnp.where(first, 0.0, prev_ref[...].astype(F32))
    nxt = jnp.where(last, 0.0, next_ref[...].astype(F32))
    row = lax.broadcasted_iota(jnp.int32, cur.shape, 0)
    w = w_ref[...]
    acc = cur * w[2:3, :] + b_ref[...]
    s = jnp.where(row == 0, prev[CONV_HALO - 1:CONV_HALO, :], pltpu.roll(cur, 1, 0))
    acc += s * w[1:2, :]
    s = jnp.where(row == 0, prev[CONV_HALO - 2:CONV_HALO - 1, :],
                  jnp.where(row == 1, prev[CONV_HALO - 1:CONV_HALO, :], pltpu.roll(cur, 2, 0)))
    acc += s * w[0:1, :]
    s = jnp.where(row == n - 1, nxt[0:1, :], pltpu.roll(cur, n - 1, 0))
    acc += s * w[3:4, :]
    s = jnp.where(row == n - 1, nxt[1:2, :],
                  jnp.where(row == n - 2, nxt[0:1, :], pltpu.roll(cur, n - 2, 0)))
    acc += s * w[4:5, :]
    o_ref[...] = _silu(acc).astype(BF16)


def _conv(proj, conv_w, conv_b, *, layer, lat_rows, seq_len):
    rows = proj.shape[0]
    halo_per_tile = CONV_ROWS // CONV_HALO
    n_halo = rows // CONV_HALO
    col0 = COL_XBC // CONV_DIM
    return pl.pallas_call(
        functools.partial(_conv_kernel, lat_tiles=lat_rows // CONV_ROWS, tiles_per_seq=seq_len // CONV_ROWS),
        grid=(rows // CONV_ROWS,),
        in_specs=[
            pl.BlockSpec((CONV_HALO, CONV_DIM), lambda k: (jnp.maximum(k * halo_per_tile - 1, 0), col0)),
            pl.BlockSpec((CONV_ROWS, CONV_DIM), lambda k: (k, col0)),
            pl.BlockSpec((CONV_HALO, CONV_DIM), lambda k: (jnp.minimum((k + 1) * halo_per_tile, n_halo - 1), col0)),
            pl.BlockSpec((None, D_CONV, CONV_DIM), lambda k: (layer, 0, 0)),
            pl.BlockSpec((None, 1, CONV_DIM), lambda k: (layer, 0, 0)),
        ],
        out_specs=pl.BlockSpec((CONV_ROWS, CONV_DIM), lambda k: (k, 0)),
        out_shape=jax.ShapeDtypeStruct((rows, CONV_DIM), BF16),
        compiler_params=_cparams(("parallel",)),
        name="conv",
    )(proj, proj, proj, conv_w, conv_b.reshape(DEPTH, 1, CONV_DIM))


def _qk_kernel(q_ref, k_ref, g_ref, cos_ref, sin_ref, o_ref, *, lat_tiles):
    is_lat = pl.program_id(0) < lat_tiles
    cos = jnp.where(is_lat, cos_ref[...], 1.0)
    sin = jnp.where(is_lat, sin_ref[...], 0.0)
    lane = lax.broadcasted_iota(jnp.int32, cos.shape, 1)
    half = AXIS_ROPE_DIM // 2
    first_half = (lane % AXIS_ROPE_DIM) < half
    q_scale = HEAD_DIM ** -0.5 * math.log2(math.e)
    for h in range(N_Q_HEADS + N_KV_HEADS):
        if h < N_Q_HEADS:
            x = q_ref[:, h * HEAD_DIM:(h + 1) * HEAD_DIM].astype(F32)
            gain = g_ref[0:1, :] * q_scale
        else:
            x = k_ref[:, (h - N_Q_HEADS) * HEAD_DIM:(h - N_Q_HEADS + 1) * HEAD_DIM].astype(F32)
            gain = g_ref[1:2, :]
        xn = x * lax.rsqrt(jnp.mean(x * x, axis=-1, keepdims=True) + RMS_EPS) * gain
        partner = jnp.where(first_half, pltpu.roll(xn, HEAD_DIM - half, 1), pltpu.roll(xn, half, 1))
        o_ref[:, h * HEAD_DIM:(h + 1) * HEAD_DIM] = (xn * cos + partner * sin).astype(BF16)


def _qk_prep(proj, gains, cos_t, sin_t, *, lat_rows, seq_len):
    rows = proj.shape[0]
    width = ATTN_DIM + KV_DIM
    tiles_per_seq = seq_len // TM
    return pl.pallas_call(
        functools.partial(_qk_kernel, lat_tiles=lat_rows // TM),
        grid=(rows // TM,),
        in_specs=[
            pl.BlockSpec((TM, ATTN_DIM), lambda i: (i, COL_Q // ATTN_DIM)),
            pl.BlockSpec((TM, KV_DIM), lambda i: (i, COL_K // KV_DIM)),
            pl.BlockSpec((2, HEAD_DIM), lambda i: (0, 0)),
            pl.BlockSpec((TM, HEAD_DIM), lambda i: (i % tiles_per_seq, 0)),
            pl.BlockSpec((TM, HEAD_DIM), lambda i: (i % tiles_per_seq, 0)),
        ],
        out_specs=pl.BlockSpec((TM, width), lambda i: (i, 0)),
        out_shape=jax.ShapeDtypeStruct((rows, width), BF16),
        compiler_params=_cparams(("parallel",)),
        name="qk_prep",
    )(proj, proj, gains, cos_t, sin_t)


def _rope_tables(seq_len):
    t = jnp.arange(seq_len)
    row = (t // GRID_W).astype(F32)
    col = (t % GRID_W).astype(F32)
    inv_freq = ROPE_THETA ** (-jnp.arange(0, AXIS_ROPE_DIM, 2, dtype=F32) / AXIS_ROPE_DIM)
    ang_r = row[:, None] * inv_freq
    ang_c = col[:, None] * inv_freq
    cos_t = jnp.concatenate([jnp.cos(ang_r), jnp.cos(ang_r), jnp.cos(ang_c), jnp.cos(ang_c)], axis=-1)
    sin_t = jnp.concatenate([-jnp.sin(ang_r), jnp.sin(ang_r), -jnp.sin(ang_c), jnp.sin(ang_c)], axis=-1)
    return cos_t, sin_t


def _with_ones(v):
    return jnp.concatenate([v, jnp.ones_like(v)], axis=1)


def _attn_kernel(q_ref, kl_ref, kc_ref, vl_ref, vc_ref, o_ref, s_ref, ve_ref, *, nq, ctx_len, seq_len):
    qi = pl.program_id(2)
    chunks = [(0, ctx_len)] + [(ctx_len + c * TK, TK) for c in range(seq_len // TK)]

    @pl.when(qi == 0)
    def _():
        ve_ref[0:ctx_len, :] = _with_ones(vc_ref[...])
        ve_ref[ctx_len:, :] = _with_ones(vl_ref[...])

    def keys(lo, n):
        return kc_ref[...] if lo == 0 else kl_ref[lo - ctx_len:lo - ctx_len + n, :]

    def scores(h, lo, n, m_lanes):
        s = _dot_nt(q_ref[:, h * HEAD_DIM:(h + 1) * HEAD_DIM], keys(lo, n))
        s_ref[h % 2, :, lo:lo + n] = s
        for k in range(n // HEAD_DIM):
            blk = s[:, k * HEAD_DIM:(k + 1) * HEAD_DIM]
            m_lanes = blk if m_lanes is None else jnp.maximum(m_lanes, blk)
        return m_lanes

    def weighted(h, lo, n, m, o_ext):
        p = jnp.exp2(s_ref[h % 2, :, lo:lo + n] - m).astype(BF16)
        part = _dot(p, ve_ref[lo:lo + n, :])
        return part if o_ext is None else o_ext + part

    def finish(h, o_ext):
        o_ref[:, h * HEAD_DIM:(h + 1) * HEAD_DIM] = (o_ext[:, :HEAD_DIM] / o_ext[:, HEAD_DIM:]).astype(BF16)

    def run(chunk_list):
        m_lanes = None
        for lo, n in chunk_list:
            m_lanes = scores(0, lo, n, m_lanes)
        for h in range(Q_PER_KV):
            m = jnp.max(m_lanes, axis=-1, keepdims=True)
            m_lanes, o_ext = None, None
            for lo, n in chunk_list:
                o_ext = weighted(h, lo, n, m, o_ext)
                if h + 1 < Q_PER_KV:
                    m_lanes = scores(h + 1, lo, n, m_lanes)
            finish(h, o_ext)

    @pl.when(qi < nq)
    def _():
        run(chunks)

    @pl.when(qi == nq)
    def _():
        run(chunks[:1])


def _attention(qk, proj, *, n_batch, seq_len, ctx_len):
    rows = qk.shape[0]
    nq = seq_len // TQ
    ctx_blk0 = n_batch * seq_len // ctx_len
    kcol = N_Q_HEADS
    vcol = COL_V // HEAD_DIM
    qw = Q_PER_KV * HEAD_DIM

    def q_map(b, g, qi):
        return (jnp.where(qi < nq, b * nq + qi, n_batch * nq + b), g)

    return pl.pallas_call(
        functools.partial(_attn_kernel, nq=nq, ctx_len=ctx_len, seq_len=seq_len),
        grid=(n_batch, N_KV_HEADS, nq + 1),
        in_specs=[
            pl.BlockSpec((TQ, qw), q_map),
            pl.BlockSpec((seq_len, HEAD_DIM), lambda b, g, qi: (b, kcol + g)),
            pl.BlockSpec((ctx_len, HEAD_DIM), lambda b, g, qi: (ctx_blk0 + b, kcol + g)),
            pl.BlockSpec((seq_len, HEAD_DIM), lambda b, g, qi: (b, vcol + g)),
            pl.BlockSpec((ctx_len, HEAD_DIM), lambda b, g, qi: (ctx_blk0 + b, vcol + g)),
        ],
        out_specs=pl.BlockSpec((TQ, qw), q_map),
        out_shape=jax.ShapeDtypeStruct((rows, ATTN_DIM), BF16),
        scratch_shapes=[pltpu.VMEM((2, TQ, ctx_len + seq_len), F32),
                        pltpu.VMEM((ctx_len + seq_len, 2 * HEAD_DIM), BF16)],
        compiler_params=_cparams(("parallel", "parallel", "arbitrary")),
        name="attention",
    )(qk, qk, qk, proj, proj)


def _ssd_expand_table():
    er = jnp.arange(2 * DT_W)[:, None] % DT_W
    ec = jnp.arange(3 * SSD_DIM)[None, :]
    return (er == (ec // SSD_DIM) * SSD_HEADS + (ec % SSD_DIM) // SSD_HEAD_DIM).astype(BF16)


def _ssd_chunk(fwd, x_ref, b_ref, c_ref, dt_ref, dtt_ref, bias_ref, biast_ref, alog_ref, alogt_ref,
               expand_ref, y_ref, st_ref):
    nh = SSD_HEADS
    gw = HEADS_PER_GROUP * SSD_HEAD_DIM
    n = SSD_CHUNK

    row = lax.broadcasted_iota(jnp.int32, (n, n), 0)
    col = lax.broadcasted_iota(jnp.int32, (n, n), 1)
    mask = (col <= row) if fwd else (col >= row)
    tri = mask.astype(BF16)
    tri_t = ((row <= col) if fwd else (row >= col)).astype(BF16)

    dt_raw = dt_ref[...] if fwd else pltpu.roll(dt_ref[...], DT_W - nh, 1)
    dtv = _softplus(dt_raw + bias_ref[...])
    da = dtv * (-jnp.exp(alog_ref[...]))
    da3 = _split3(da)
    cs = _dot(tri, da3[0]) + _dot(tri, da3[1]) + _dot(tri, da3[2])
    total = cs[n - 1:n, :] if fwd else cs[0:1, :]

    dtv_t = _softplus(dtt_ref[...] + biast_ref[...])
    da_t = dtv_t * (-jnp.exp(alogt_ref[...]))
    da_t3 = _split3(da_t)
    cs_t = _dot(da_t3[0], tri_t) + _dot(da_t3[1], tri_t) + _dot(da_t3[2], tri_t)

    decay_end = jnp.exp(total - cs)
    ecs = jnp.exp(cs)
    lane = lax.broadcasted_iota(jnp.int32, (n, DT_W), 1)
    packed = jnp.where(lane < nh, dtv,
                       jnp.where(lane < 2 * nh, pltpu.roll(dtv * decay_end, nh, 1),
                                 jnp.where(lane < 3 * nh, pltpu.roll(ecs, 2 * nh, 1), 0.0)))
    ex = _dot(jnp.concatenate(_split2(packed), axis=1), expand_ref[...])
    dt_x = ex[:, 0:SSD_DIM]
    dtdec_x = ex[:, SSD_DIM:2 * SSD_DIM]
    ecs_x = ex[:, 2 * SSD_DIM:3 * SSD_DIM]

    xf = x_ref[...].astype(F32)
    xdt = (xf * dt_x).astype(BF16)
    xdec = (xf * dtdec_x).astype(BF16)
    lane_h = lax.broadcasted_iota(jnp.int32, (n, 2 * SSD_HEAD_DIM), 1)

    for g in range(SSD_GROUPS):
        bg = b_ref[:, g * SSD_STATE:(g + 1) * SSD_STATE]
        cg = c_ref[:, g * SSD_STATE:(g + 1) * SSD_STATE]
        cb = _dot_nt(cg, bg)
        st = st_ref[g]
        y_off = _dot(cg, st.astype(BF16)) * ecs_x[:, g * gw:(g + 1) * gw]
        for j in range(HEADS_PER_GROUP // 2):
            ws = []
            for e in (2 * j, 2 * j + 1):
                hcol = g * HEADS_PER_GROUP + e
                diff = cs[:, hcol:hcol + 1] - cs_t[hcol:hcol + 1, :]
                lm = jnp.exp(jnp.where(mask, diff, -1e30))
                ws.append((cb * lm).astype(BF16))
            w_pair = jnp.concatenate(ws, axis=1)
            lo = g * gw + j * 2 * SSD_HEAD_DIM
            x2 = xdt[:, lo:lo + 2 * SSD_HEAD_DIM]
            rhs = jnp.concatenate([jnp.where(lane_h < SSD_HEAD_DIM, x2, jnp.zeros_like(x2)),
                                   jnp.where(lane_h >= SSD_HEAD_DIM, x2, jnp.zeros_like(x2))], axis=0)
            y_ref[:, lo:lo + 2 * SSD_HEAD_DIM] = (
                _dot(w_pair, rhs) + y_off[:, j * 2 * SSD_HEAD_DIM:(j + 1) * 2 * SSD_HEAD_DIM])
        new_states = _dot_tn(bg, xdec[:, g * gw:(g + 1) * gw])
        etot = ecs_x[n - 1:n, g * gw:(g + 1) * gw] if fwd else ecs_x[0:1, g * gw:(g + 1) * gw]
        st_ref[g] = st * etot + new_states


def _ssd_kernel(*refs):
    n_in = 9
    expand_ref = refs[2 * n_in]
    st_ref = refs[-1]

    @pl.when(pl.program_id(1) == 0)
    def _():
        st_ref[...] = jnp.zeros_like(st_ref)

    for d in range(2):
        _ssd_chunk(d == 0, *refs[d * n_in:(d + 1) * n_in], expand_ref, refs[2 * n_in + 1 + d], st_ref.at[d])


def _ssd(xbc, dt, dt_t, bias, bias_t, alog, alog_t, *, layer, n_batch, seq_len, ctx_len):
    rows = xbc.shape[0]
    ncc = ctx_len // SSD_CHUNK
    ncl = seq_len // SSD_CHUNK
    lat_blk = seq_len // SSD_CHUNK
    ctx_blk0 = n_batch * seq_len // SSD_CHUNK
    bcol = SSD_DIM // (SSD_GROUPS * SSD_STATE)

    def rb(d):
        def block(b, c):
            cc = c if d == 0 else ncc - 1 - c
            lc = c - ncc if d == 0 else ncl - 1 - (c - ncc)
            return jnp.where(c < ncc, ctx_blk0 + b * ncc + cc, b * lat_blk + lc)
        return block

    def operand_specs(d):
        blk = rb(d)
        return [
            pl.BlockSpec((SSD_CHUNK, SSD_DIM), lambda b, c: (blk(b, c), 0)),
            pl.BlockSpec((SSD_CHUNK, SSD_GROUPS * SSD_STATE), lambda b, c: (blk(b, c), bcol)),
            pl.BlockSpec((SSD_CHUNK, SSD_GROUPS * SSD_STATE), lambda b, c: (blk(b, c), bcol + 1)),
            pl.BlockSpec((SSD_CHUNK, DT_W), lambda b, c: (blk(b, c), 0)),
            pl.BlockSpec((SSD_HEADS, SSD_CHUNK), lambda b, c: (d, blk(b, c))),
            pl.BlockSpec((None, None, 1, DT_W), lambda b, c: (layer, d, 0, 0)),
            pl.BlockSpec((None, None, SSD_HEADS, 1), lambda b, c: (layer, d, 0, 0)),
            pl.BlockSpec((None, None, 1, DT_W), lambda b, c: (layer, d, 0, 0)),
            pl.BlockSpec((None, None, SSD_HEADS, 1), lambda b, c: (layer, d, 0, 0)),
        ]

    operands = (xbc, xbc, xbc, dt, dt_t, bias, bias_t, alog, alog_t)
    y_shape = jax.ShapeDtypeStruct((rows, SSD_DIM), F32)
    return pl.pallas_call(
        _ssd_kernel,
        grid=(n_batch, ncc + ncl),
        in_specs=operand_specs(0) + operand_specs(1) + [
            pl.BlockSpec((2 * DT_W, 3 * SSD_DIM), lambda b, c: (0, 0))],
        out_specs=[pl.BlockSpec((SSD_CHUNK, SSD_DIM), lambda b, c: (rb(0)(b, c), 0)),
                   pl.BlockSpec((SSD_CHUNK, SSD_DIM), lambda b, c: (rb(1)(b, c), 0))],
        out_shape=[y_shape, y_shape],
        scratch_shapes=[pltpu.VMEM((2, SSD_GROUPS, SSD_STATE, HEADS_PER_GROUP * SSD_HEAD_DIM), F32)],
        compiler_params=_cparams(("parallel", "arbitrary")),
        name="ssd",
    )(*operands, *operands, _ssd_expand_table())


def _fourier_fold(u_refs, chan_ref, p_ref, ah_ref, n):
    gd = FOURIER_GROUP_DIM
    half = n // 2
    bs = min(256, half)
    nb = half // bs
    r_i = lax.broadcasted_iota(jnp.int32, (bs, bs), 0)
    c_i = lax.broadcasted_iota(jnp.int32, (bs, bs), 1)
    flip = (r_i + c_i == bs).astype(BF16)
    row0 = lax.broadcasted_iota(jnp.int32, (bs, gd), 0) == 0
    cos_c = chan_ref[:, :gd]
    sin_c = chan_ref[:, gd:]
    for g, u_ref in enumerate(u_refs):
        cols = slice(g * gd, (g + 1) * gd)
        for i in range(nb):
            lo = u_ref[bs * i:bs * (i + 1), :].astype(F32)
            rev = _dot(flip, u_ref[bs * (2 * nb - 1 - i):bs * (2 * nb - i), :])
            if i > 0:
                first = u_ref[bs * (2 * nb - i):bs * (2 * nb - i) + 16, :][0:1, :].astype(F32)
                rev = jnp.where(row0, first, rev)
            p_ref[bs * i:bs * (i + 1), cols] = _dot((lo + rev).astype(BF16), cos_c).astype(BF16)
            p_ref[half + bs * i:half + bs * (i + 1), cols] = _dot((lo - rev).astype(BF16), sin_c).astype(BF16)
        ah_ref[:, cols] = _dot(u_ref[half:half + 16, :], cos_c)


def _fourier_rows(m_ref, p_ref, ah_ref, o_ref, n):
    rows = m_ref.shape[0]
    parity = lax.broadcasted_iota(jnp.int32, (rows, 1), 0) & 1
    sign = (1 - 2 * parity).astype(F32)
    nyquist = ah_ref[0:1, :] * (n ** -0.5)
    o_ref[...] = (_dot(m_ref[...], p_ref[0:n, :]) + sign * nyquist).astype(BF16)


def _fourier_kernel(ul0, ul1, ul2, ul3, uc0, uc1, uc2, uc3, chan_ref, ml_ref, mc_ref, o_ref, p_ref, ah_ref, *,
                    nm, seq_len, ctx_len):
    mi = pl.program_id(1)

    @pl.when(mi == 0)
    def _():
        _fourier_fold((ul0, ul1, ul2, ul3), chan_ref, p_ref, ah_ref, seq_len)

    @pl.when(mi < nm)
    def _():
        _fourier_rows(ml_ref, p_ref, ah_ref, o_ref, seq_len)

    @pl.when(mi == nm)
    def _():
        _fourier_fold((uc0, uc1, uc2, uc3), chan_ref, p_ref, ah_ref, ctx_len)
        _fourier_rows(mc_ref, p_ref, ah_ref, o_ref, ctx_len)


def _fourier(proj, chan_mat, seq_mat_l, seq_mat_c, *, n_batch, seq_len, ctx_len):
    rows = proj.shape[0]
    tm = TM_FOURIER
    nm = seq_len // tm
    gd = FOURIER_GROUP_DIM
    col0 = COL_U // gd
    ctx_blk0 = n_batch * seq_len // ctx_len

    def u_specs(block_rows, blk0):
        return [pl.BlockSpec((block_rows, gd), functools.partial(lambda b, mi, g: (blk0 + b, col0 + g), g=g))
                for g in range(FOURIER_GROUPS)]

    def out_map(b, mi):
        return (jnp.where(mi < nm, b * nm + mi, n_batch * nm + b), 0)

    return pl.pallas_call(
        functools.partial(_fourier_kernel, nm=nm, seq_len=seq_len, ctx_len=ctx_len),
        grid=(n_batch, nm + 1),
        in_specs=u_specs(seq_len, 0) + u_specs(ctx_len, ctx_blk0) + [
            pl.BlockSpec((gd, 2 * gd), lambda b, mi: (0, 0)),
            pl.BlockSpec((tm, seq_len), lambda b, mi: (jnp.minimum(mi, nm - 1), 0)),
            pl.BlockSpec((ctx_len, ctx_len), lambda b, mi: (0, 0)),
        ],
        out_specs=pl.BlockSpec((tm, FOURIER_DIM), out_map),
        out_shape=jax.ShapeDtypeStruct((rows, FOURIER_DIM), BF16),
        scratch_shapes=[pltpu.VMEM((seq_len, FOURIER_DIM), BF16), pltpu.VMEM((16, FOURIER_DIM), F32)],
        compiler_params=_cparams(("parallel", "arbitrary")),
        name="fourier",
    )(*([proj] * (2 * FOURIER_GROUPS)), chan_mat, seq_mat_l, seq_mat_c)


def _dft_cos_sin(n, n_cols):
    j = jnp.arange(n, dtype=jnp.int32)
    jk = (j[:, None] * j[None, :n_cols]) % n
    ang = jk.astype(F32) * (2.0 * math.pi / n)
    return jnp.cos(ang), jnp.sin(ang)


def _dft_tables(n):
    cos_m, sin_m = _dft_cos_sin(n, n // 2)
    return (jnp.concatenate([cos_m, -sin_m], axis=1) * (n ** -0.5)).astype(BF16)


def _chan_table():
    cos_m, sin_m = _dft_cos_sin(FOURIER_GROUP_DIM, FOURIER_GROUP_DIM)
    return (jnp.concatenate([cos_m, sin_m], axis=1) * (FOURIER_GROUP_DIM ** -0.5)).astype(BF16)


def _merge_kernel(x_ref, mod_ref, yf_ref, yb_ref, xh_ref, z_ref, four_ref, attn_ref, g_ref, wb_ref, wo_ref,
                  dsk_ref, ng_ref, lg_ref, lb_ref, o_ref):
    y = yf_ref[...] + yb_ref[...] + dsk_ref[...] * xh_ref[...].astype(F32)
    y = y * _silu(z_ref[...].astype(F32))
    gw = SSD_DIM // SSD_GROUPS
    heads = []
    for g in range(SSD_GROUPS):
        yg = y[:, g * gw:(g + 1) * gw]
        yg = yg * lax.rsqrt(jnp.mean(yg * yg, axis=-1, keepdims=True) + RMS_EPS)
        heads.append((yg * ng_ref[:, g * gw:(g + 1) * gw]).astype(BF16))
    branches = (jnp.concatenate(heads, axis=1), four_ref[...], attn_ref[...])

    acc = None
    for j in range(D_MODEL // TN_MERGE):
        lo = j * TN_MERGE
        m = None
        for k, yk in enumerate(branches):
            gate = jax.nn.sigmoid(g_ref[:, k * D_MODEL + lo:k * D_MODEL + lo + TN_MERGE].astype(F32))
            term = gate * _dot(yk, wb_ref[k, :, lo:lo + TN_MERGE])
            m = term if m is None else m + term
        part = _dot(m.astype(BF16), wo_ref[lo:lo + TN_MERGE, :])
        acc = part if acc is None else acc + part

    v = ALPHA * x_ref[...] + mod_ref[5:6, :] * acc
    o_ref[...] = _layer_norm(v, lg_ref[...], lb_ref[...])


def _merge(xs, rows, mods, y_ssd, xbc, proj, four, attn, w_branch, w_out, dsk, norm_g, ln_g, ln_b, *,
           layer, seq_len, n_batch):
    tiles_per_batch = seq_len // TM_MERGE
    resident = pl.Buffered(1)
    row_spec = pl.BlockSpec((TM_MERGE, BRANCH_DIM), lambda i: (i, 0))
    vec_spec = pl.BlockSpec((None, 1, BRANCH_DIM), lambda i: (layer, 0, 0))
    ln_spec = pl.BlockSpec((None, None, 1, D_MODEL), lambda i: (layer, 1, 0, 0))
    return pl.pallas_call(
        _merge_kernel,
        grid=(rows // TM_MERGE,),
        in_specs=[
            pl.BlockSpec((TM_MERGE, D_MODEL), lambda i: (i, 0)),
            pl.BlockSpec((None, None, N_MOD, D_MODEL),
                         lambda i: (layer, jnp.minimum(i // tiles_per_batch, n_batch), 0, 0)),
            row_spec,
            row_spec,
            row_spec,
            pl.BlockSpec((TM_MERGE, BRANCH_DIM), lambda i: (i, COL_Z // BRANCH_DIM)),
            row_spec,
            row_spec,
            pl.BlockSpec((TM_MERGE, N_BRANCH * D_MODEL), lambda i: (i, COL_G // (N_BRANCH * D_MODEL))),
            pl.BlockSpec((None, N_BRANCH, BRANCH_DIM, D_MODEL), lambda i: (layer, 0, 0, 0), pipeline_mode=resident),
            pl.BlockSpec((None, D_MODEL, D_MODEL), lambda i: (layer, 0, 0), pipeline_mode=resident),
            vec_spec, vec_spec,
            ln_spec, ln_spec,
        ],
        out_specs=pl.BlockSpec((TM_MERGE, D_MODEL), lambda i: (i, 0)),
        out_shape=jax.ShapeDtypeStruct((rows, D_MODEL), F32),
        compiler_params=_cparams(("parallel",)),
        name="merge",
    )(xs, mods, y_ssd[0], y_ssd[1], xbc, proj, four, attn, proj, w_branch, w_out, dsk, norm_g, ln_g, ln_b)


def kernel(x, c, ctx, c_ctx, w_ada, b_ada, ln_g, ln_b, ffn_wgu, ffn_wd, w_in, conv_w, conv_b, dt_bias, a_log,
           d_skip, ssd_norm_g, q_norm_g, k_norm_g, w_branch, w_out):
    n_batch, seq_len, _ = x.shape
    ctx_len = ctx.shape[1]
    lat_rows = n_batch * seq_len
    rows = lat_rows + n_batch * ctx_len
    assert ctx_len == TQ and seq_len % TM == 0 and (n_batch * ctx_len) % TM == 0 and n_batch < MOD_ROWS
    tiles_per_batch = seq_len // TM
    tile_kw = dict(tiles_per_batch=tiles_per_batch, n_batch=n_batch)

    wgu_bf = ffn_wgu.astype(BF16)
    wd_bf = ffn_wd.astype(BF16)
    wb_bf = w_branch.astype(BF16)
    wo_bf = w_out.astype(BF16)
    w_z, w_xbc, w_dtf, w_dtb, w_u, w_q, w_k, w_v, w_g = jnp.split(w_in, IN_SPLITS, axis=-1)
    w_main = jnp.concatenate([w_g, w_z, w_u, w_q, w_xbc, w_k, w_v], axis=-1).astype(BF16)
    w_dt = jnp.concatenate([w_dtf, w_dtb, jnp.zeros((DEPTH, D_MODEL, DT_W - 2 * SSD_HEADS), F32)],
                           axis=-1).astype(BF16)
    head_pad = ((0, 0), (0, 0), (0, DT_W - SSD_HEADS))
    bias_row = jnp.pad(dt_bias, head_pad).reshape(DEPTH, 2, 1, DT_W)
    alog_row = jnp.pad(a_log, head_pad).reshape(DEPTH, 2, 1, DT_W)
    bias_col = dt_bias.reshape(DEPTH, 2, SSD_HEADS, 1)
    alog_col = a_log.reshape(DEPTH, 2, SSD_HEADS, 1)
    dsk_all = jnp.repeat(d_skip, SSD_HEAD_DIM, axis=-1).reshape(DEPTH, 1, SSD_DIM)
    norm_g_all = ssd_norm_g.reshape(DEPTH, 1, SSD_DIM)
    qk_gains = jnp.stack([q_norm_g, k_norm_g], axis=1)
    ln_g4 = ln_g.reshape(DEPTH, 3, 1, D_MODEL)
    ln_b4 = ln_b.reshape(DEPTH, 3, 1, D_MODEL)
    cos_t, sin_t = _rope_tables(seq_len)
    chan_mat = _chan_table()
    seq_mat_l = _dft_tables(seq_len)
    seq_mat_c = _dft_tables(ctx_len)

    cvec = jnp.concatenate([c, c_ctx[None], jnp.zeros((MOD_ROWS - n_batch - 1, D_MODEL), F32)], axis=0)
    mods = _adaln_all(cvec, w_ada, b_ada)

    xs = jnp.concatenate([x.reshape(lat_rows, D_MODEL), ctx.reshape(n_batch * ctx_len, D_MODEL)], axis=0)
    for layer in range(DEPTH):
        last = layer == DEPTH - 1
        xs = _ffn(xs, rows, mods, wgu_bf, wd_bf, ln_g4, ln_b4, layer=layer, which=0, sub=0, **tile_kw)

        proj, dt = _inproj(xs, mods, w_main, w_dt, layer=layer, **tile_kw)
        xbc = _conv(proj, conv_w, conv_b, layer=layer, lat_rows=lat_rows, seq_len=seq_len)
        y_ssd = _ssd(xbc, dt, dt.T, bias_row, bias_col, alog_row, alog_col,
                     layer=layer, n_batch=n_batch, seq_len=seq_len, ctx_len=ctx_len)
        qk = _qk_prep(proj, qk_gains[layer], cos_t, sin_t, lat_rows=lat_rows, seq_len=seq_len)
        attn = _attention(qk, proj, n_batch=n_batch, seq_len=seq_len, ctx_len=ctx_len)
        four = _fourier(proj, chan_mat, seq_mat_l, seq_mat_c, n_batch=n_batch, seq_len=seq_len, ctx_len=ctx_len)
        out_rows = lat_rows if last else rows
        xs = _merge(xs, out_rows, mods, y_ssd, xbc, proj, four, attn, wb_bf, wo_bf, dsk_all, norm_g_all,
                    ln_g4, ln_b4, layer=layer, seq_len=seq_len, n_batch=n_batch)
        xs = _ffn(xs, out_rows, mods, wgu_bf, wd_bf, ln_g4, ln_b4, layer=layer, which=1, sub=2, **tile_kw)
    return xs.reshape(n_batch, seq_len, D_MODEL)
```

```python
import functools
import math

import jax
import jax.numpy as jnp
from jax import lax
from jax.experimental import pallas as pl
from jax.experimental.pallas import tpu as pltpu

F32 = jnp.float32
BF16 = jnp.bfloat16

D_MODEL = 2048
DEPTH = 4
GRID_W = 64
N_MOD = 9
ALPHA = (2 * DEPTH) ** 0.25
LN_EPS = 1e-6
RMS_EPS = 1e-6
D_FF = 5632
BRANCH_DIM = D_MODEL // 2
N_BRANCH = 3
SSD_DIM = BRANCH_DIM
SSD_HEAD_DIM = 64
SSD_HEADS = SSD_DIM // SSD_HEAD_DIM
SSD_GROUPS = 2
HEADS_PER_GROUP = SSD_HEADS // SSD_GROUPS
SSD_STATE = 128
SSD_CHUNK = 128
D_CONV = 5
CONV_DIM = SSD_DIM + 2 * SSD_GROUPS * SSD_STATE
FOURIER_DIM = BRANCH_DIM
FOURIER_GROUPS = 4
FOURIER_GROUP_DIM = FOURIER_DIM // FOURIER_GROUPS
HEAD_DIM = 128
N_Q_HEADS = BRANCH_DIM // HEAD_DIM
N_KV_HEADS = 2
Q_PER_KV = N_Q_HEADS // N_KV_HEADS
ATTN_DIM = N_Q_HEADS * HEAD_DIM
KV_DIM = N_KV_HEADS * HEAD_DIM
AXIS_ROPE_DIM = HEAD_DIM // 2
ROPE_THETA = 10000.0
IN_WIDTHS = (SSD_DIM, CONV_DIM, SSD_HEADS, SSD_HEADS, FOURIER_DIM, ATTN_DIM, KV_DIM, KV_DIM, N_BRANCH * D_MODEL)
IN_SPLITS = tuple(sum(IN_WIDTHS[:i + 1]) for i in range(len(IN_WIDTHS) - 1))

COL_G = 0
COL_Z = COL_G + N_BRANCH * D_MODEL
COL_U = COL_Z + SSD_DIM
COL_Q = COL_U + FOURIER_DIM
COL_XBC = COL_Q + ATTN_DIM
COL_K = COL_XBC + CONV_DIM
COL_V = COL_K + KV_DIM
PROJ_W = COL_V + KV_DIM
DT_W = 128
MOD_ROWS = 16

TM = 512
TF = 512
TN_PROJ = 2816
FFN_SPLIT = 2
TM_MERGE = 256
TN_MERGE = 512
TN_ADA = 1024
CONV_ROWS = 256
CONV_HALO = 16
TQ = 256
TK = 512
TM_FOURIER = 256
VMEM_LIMIT = 56 * 1024 * 1024


def _cparams(sem):
    return pltpu.CompilerParams(dimension_semantics=sem, vmem_limit_bytes=VMEM_LIMIT)


def _silu(x):
    return x * jax.nn.sigmoid(x)


def _softplus(x):
    return jnp.maximum(x, 0.0) + jnp.log1p(jnp.exp(-jnp.abs(x)))


def _layer_norm(v, g, b):
    mu = jnp.mean(v, axis=-1, keepdims=True)
    d = v - mu
    var = jnp.mean(d * d, axis=-1, keepdims=True)
    return d * lax.rsqrt(var + LN_EPS) * g + b


def _dot(a, b):
    return jnp.dot(a, b, preferred_element_type=F32)


def _dot_nt(a, b):
    return lax.dot_general(a, b, (((1,), (1,)), ((), ())), preferred_element_type=F32)


def _dot_tn(a, b):
    return lax.dot_general(a, b, (((0,), (0,)), ((), ())), preferred_element_type=F32)


def _split2(a):
    hi = a.astype(BF16)
    lo = (a - hi.astype(F32)).astype(BF16)
    return hi, lo


def _split3(a):
    hi = a.astype(BF16)
    r = a - hi.astype(F32)
    mid = r.astype(BF16)
    lo = (r - mid.astype(F32)).astype(BF16)
    return hi, mid, lo


def _ada_kernel(c_ref, w_ref, b_ref, o_ref):
    h = _silu(c_ref[...]).astype(BF16)
    o_ref[...] = _dot(h, w_ref[...].astype(BF16)) + b_ref[...]


def _adaln_all(cvec, w_ada, b_ada):
    n = N_MOD * D_MODEL
    out = pl.pallas_call(
        _ada_kernel,
        grid=(DEPTH, n // TN_ADA),
        in_specs=[
            pl.BlockSpec((MOD_ROWS, D_MODEL), lambda l, j: (0, 0)),
            pl.BlockSpec((None, D_MODEL, TN_ADA), lambda l, j: (l, 0, j)),
            pl.BlockSpec((None, 1, TN_ADA), lambda l, j: (l, 0, j)),
        ],
        out_specs=pl.BlockSpec((None, MOD_ROWS, TN_ADA), lambda l, j: (l, 0, j)),
        out_shape=jax.ShapeDtypeStruct((DEPTH, MOD_ROWS, n), F32),
        compiler_params=_cparams(("parallel", "parallel")),
        name="adaln",
    )(cvec, w_ada, b_ada.reshape(DEPTH, 1, n))
    return out.reshape(DEPTH, MOD_ROWS, N_MOD, D_MODEL)


def _mod_spec(layer, tiles_per_batch, n_batch):
    return pl.BlockSpec((None, None, N_MOD, D_MODEL),
                        lambda i, j: (layer, jnp.minimum(i // tiles_per_batch, n_batch), 0, 0))


def _ln_spec(layer, sub):
    return pl.BlockSpec((None, None, 1, D_MODEL), lambda i, j: (layer, sub, 0, 0))


def _ffn_kernel(x_ref, mod_ref, wg_ref, wu_ref, wd_ref, g_ref, b_ref, o_ref, h_ref, *, sub, nj):
    j = pl.program_id(1)
    cols = TF // FFN_SPLIT

    def step(first, last):
        if first:
            shift = mod_ref[3 * sub:3 * sub + 1, :]
            scale = mod_ref[3 * sub + 1:3 * sub + 2, :]
            h = (x_ref[...] * (1.0 + scale) + shift).astype(BF16)
            h_ref[...] = h
        else:
            h = h_ref[...]
        acc = None
        for c in range(FFN_SPLIT):
            cs = slice(c * cols, (c + 1) * cols)
            act = (_silu(_dot(h, wg_ref[:, cs])) * _dot(h, wu_ref[:, cs])).astype(BF16)
            part = _dot(act, wd_ref[cs, :])
            acc = part if acc is None else acc + part
        if not first:
            acc = o_ref[...] + acc
        if last:
            gain = mod_ref[3 * sub + 2:3 * sub + 3, :]
            acc = _layer_norm(ALPHA * x_ref[...] + 0.5 * gain * acc, g_ref[...], b_ref[...])
        o_ref[...] = acc

    @pl.when(j == 0)
    def _():
        step(True, False)

    @pl.when(jnp.logical_and(j > 0, j < nj - 1))
    def _():
        step(False, False)

    @pl.when(j == nj - 1)
    def _():
        step(False, True)


def _ffn(xs, rows, mods, wgu, wd, ln_g, ln_b, *, layer, which, sub, tiles_per_batch, n_batch):
    nj = D_FF // TF
    return pl.pallas_call(
        functools.partial(_ffn_kernel, sub=sub, nj=nj),
        grid=(rows // TM, nj),
        in_specs=[
            pl.BlockSpec((TM, D_MODEL), lambda i, j: (i, 0)),
            _mod_spec(layer, tiles_per_batch, n_batch),
            pl.BlockSpec((None, None, D_MODEL, TF), lambda i, j: (layer, which, 0, j)),
            pl.BlockSpec((None, None, D_MODEL, TF), lambda i, j: (layer, which, 0, nj + j)),
            pl.BlockSpec((None, None, TF, D_MODEL), lambda i, j: (layer, which, j, 0)),
            _ln_spec(layer, sub),
            _ln_spec(layer, sub),
        ],
        out_specs=pl.BlockSpec((TM, D_MODEL), lambda i, j: (i, 0)),
        out_shape=jax.ShapeDtypeStruct((rows, D_MODEL), F32),
        scratch_shapes=[pltpu.VMEM((TM, D_MODEL), BF16)],
        compiler_params=_cparams(("parallel", "arbitrary")),
        name="ffn",
    )(xs, mods, wgu, wgu, wd, ln_g, ln_b)


def _inproj_kernel(x_ref, mod_ref, w_ref, wdt_ref, o_ref, dt_ref, h_ref):
    j = pl.program_id(1)

    @pl.when(j == 0)
    def _():
        shift = mod_ref[3:4, :]
        scale = mod_ref[4:5, :]
        h = (x_ref[...] * (1.0 + scale) + shift).astype(BF16)
        h_ref[...] = h
        dt_ref[...] = _dot(h, wdt_ref[...])

    o_ref[...] = _dot(h_ref[...], w_ref[...]).astype(BF16)


def _inproj(xs, mods, w_main, w_dt, *, layer, tiles_per_batch, n_batch):
    rows = xs.shape[0]
    return pl.pallas_call(
        _inproj_kernel,
        grid=(rows // TM, PROJ_W // TN_PROJ),
        in_specs=[
            pl.BlockSpec((TM, D_MODEL), lambda i, j: (i, 0)),
            _mod_spec(layer, tiles_per_batch, n_batch),
            pl.BlockSpec((None, D_MODEL, TN_PROJ), lambda i, j: (layer, 0, j)),
            pl.BlockSpec((None, D_MODEL, DT_W), lambda i, j: (layer, 0, 0)),
        ],
        out_specs=[
            pl.BlockSpec((TM, TN_PROJ), lambda i, j: (i, j)),
            pl.BlockSpec((TM, DT_W), lambda i, j: (i, 0)),
        ],
        out_shape=[
            jax.ShapeDtypeStruct((rows, PROJ_W), BF16),
            jax.ShapeDtypeStruct((rows, DT_W), F32),
        ],
        scratch_shapes=[pltpu.VMEM((TM, D_MODEL), BF16)],
        compiler_params=_cparams(("parallel", "arbitrary")),
        name="inproj",
    )(xs, mods, w_main, w_dt)


def _conv_kernel(prev_ref, cur_ref, next_ref, w_ref, b_ref, o_ref, *, lat_tiles, tiles_per_seq):
    k = pl.program_id(0)
    in_ctx = k >= lat_tiles
    pos = k % tiles_per_seq
    first = jnp.logical_or(in_ctx, pos == 0)
    last = jnp.logical_or(in_ctx, pos == tiles_per_seq - 1)
    cur = cur_ref[...].astype(F32)
    n = cur.shape[0]
    prev = jnp.where(first, 0.0, prev_ref[...].astype(F32))
    nxt = jnp.where(last, 0.0, next_ref[...].astype(F32))
    row = lax.broadcasted_iota(jnp.int32, cur.shape, 0)
    w = w_ref[...]
    acc = cur * w[2:3, :] + b_ref[...]
    s = jnp.where(row == 0, prev[CONV_HALO - 1:CONV_HALO, :], pltpu.roll(cur, 1, 0))
    acc += s * w[1:2, :]
    s = jnp.where(row == 0, prev[CONV_HALO - 2:CONV_HALO - 1, :],
                  jnp.where(row == 1, prev[CONV_HALO - 1:CONV_HALO, :], pltpu.roll(cur, 2, 0)))
    acc += s * w[0:1, :]
    s = jnp.where(row == n - 1, nxt[0:1, :], pltpu.roll(cur, n - 1, 0))
    acc += s * w[3:4, :]
    s = jnp.where(row == n - 1, nxt[1:2, :],
                  jnp.where(row == n - 2, nxt[0:1, :], pltpu.roll(cur, n - 2, 0)))
    acc += s * w[4:5, :]
    o_ref[...] = _silu(acc).astype(BF16)


def _conv(proj, conv_w, conv_b, *, layer, lat_rows, seq_len):
    rows = proj.shape[0]
    halo_per_tile = CONV_ROWS // CONV_HALO
    n_halo = rows // CONV_HALO
    col0 = COL_XBC // CONV_DIM
    return pl.pallas_call(
        functools.partial(_conv_kernel, lat_tiles=lat_rows // CONV_ROWS, tiles_per_seq=seq_len // CONV_ROWS),
        grid=(rows // CONV_ROWS,),
        in_specs=[
            pl.BlockSpec((CONV_HALO, CONV_DIM), lambda k: (jnp.maximum(k * halo_per_tile - 1, 0), col0)),
            pl.BlockSpec((CONV_ROWS, CONV_DIM), lambda k: (k, col0)),
            pl.BlockSpec((CONV_HALO, CONV_DIM), lambda k: (jnp.minimum((k + 1) * halo_per_tile, n_halo - 1), col0)),
            pl.BlockSpec((None, D_CONV, CONV_DIM), lambda k: (layer, 0, 0)),
            pl.BlockSpec((None, 1, CONV_DIM), lambda k: (layer, 0, 0)),
        ],
        out_specs=pl.BlockSpec((CONV_ROWS, CONV_DIM), lambda k: (k, 0)),
        out_shape=jax.ShapeDtypeStruct((rows, CONV_DIM), BF16),
        compiler_params=_cparams(("parallel",)),
        name="conv",
    )(proj, proj, proj, conv_w, conv_b.reshape(DEPTH, 1, CONV_DIM))


def _qk_kernel(q_ref, k_ref, g_ref, cos_ref, sin_ref, o_ref, *, lat_tiles):
    is_lat = pl.program_id(0) < lat_tiles
    cos = jnp.where(is_lat, cos_ref[...], 1.0)
    sin = jnp.where(is_lat, sin_ref[...], 0.0)
    lane = lax.broadcasted_iota(jnp.int32, cos.shape, 1)
    half = AXIS_ROPE_DIM // 2
    first_half = (lane % AXIS_ROPE_DIM) < half
    q_scale = HEAD_DIM ** -0.5 * math.log2(math.e)
    for h in range(N_Q_HEADS + N_KV_HEADS):
        if h < N_Q_HEADS:
            x = q_ref[:, h * HEAD_DIM:(h + 1) * HEAD_DIM].astype(F32)
            gain = g_ref[0:1, :] * q_scale
        else:
            x = k_ref[:, (h - N_Q_HEADS) * HEAD_DIM:(h - N_Q_HEADS + 1) * HEAD_DIM].astype(F32)
            gain = g_ref[1:2, :]
        xn = x * lax.rsqrt(jnp.mean(x * x, axis=-1, keepdims=True) + RMS_EPS) * gain
        partner = jnp.where(first_half, pltpu.roll(xn, HEAD_DIM - half, 1), pltpu.roll(xn, half, 1))
        o_ref[:, h * HEAD_DIM:(h + 1) * HEAD_DIM] = (xn * cos + partner * sin).astype(BF16)


def _qk_prep(proj, gains, cos_t, sin_t, *, lat_rows, seq_len):
    rows = proj.shape[0]
    width = ATTN_DIM + KV_DIM
    tiles_per_seq = seq_len // TM
    return pl.pallas_call(
        functools.partial(_qk_kernel, lat_tiles=lat_rows // TM),
        grid=(rows // TM,),
        in_specs=[
            pl.BlockSpec((TM, ATTN_DIM), lambda i: (i, COL_Q // ATTN_DIM)),
            pl.BlockSpec((TM, KV_DIM), lambda i: (i, COL_K // KV_DIM)),
            pl.BlockSpec((2, HEAD_DIM), lambda i: (0, 0)),
            pl.BlockSpec((TM, HEAD_DIM), lambda i: (i % tiles_per_seq, 0)),
            pl.BlockSpec((TM, HEAD_DIM), lambda i: (i % tiles_per_seq, 0)),
        ],
        out_specs=pl.BlockSpec((TM, width), lambda i: (i, 0)),
        out_shape=jax.ShapeDtypeStruct((rows, width), BF16),
        compiler_params=_cparams(("parallel",)),
        name="qk_prep",
    )(proj, proj, gains, cos_t, sin_t)


def _rope_tables(seq_len):
    t = jnp.arange(seq_len)
    row = (t // GRID_W).astype(F32)
    col = (t % GRID_W).astype(F32)
    inv_freq = ROPE_THETA ** (-jnp.arange(0, AXIS_ROPE_DIM, 2, dtype=F32) / AXIS_ROPE_DIM)
    ang_r = row[:, None] * inv_freq
    ang_c = col[:, None] * inv_freq
    cos_t = jnp.concatenate([jnp.cos(ang_r), jnp.cos(ang_r), jnp.cos(ang_c), jnp.cos(ang_c)], axis=-1)
    sin_t = jnp.concatenate([-jnp.sin(ang_r), jnp.sin(ang_r), -jnp.sin(ang_c), jnp.sin(ang_c)], axis=-1)
    return cos_t, sin_t


def _with_ones(v):
    return jnp.concatenate([v, jnp.ones_like(v)], axis=1)


def _attn_kernel(q_ref, kl_ref, kc_ref, vl_ref, vc_ref, o_ref, s_ref, ve_ref, *, nq, ctx_len, seq_len):
    qi = pl.program_id(2)
    chunks = [(0, ctx_len)] + [(ctx_len + c * TK, TK) for c in range(seq_len // TK)]

    @pl.when(qi == 0)
    def _():
        ve_ref[0:ctx_len, :] = _with_ones(vc_ref[...])
        ve_ref[ctx_len:, :] = _with_ones(vl_ref[...])

    def keys(lo, n):
        return kc_ref[...] if lo == 0 else kl_ref[lo - ctx_len:lo - ctx_len + n, :]

    def scores(h, lo, n, m_lanes):
        s = _dot_nt(q_ref[:, h * HEAD_DIM:(h + 1) * HEAD_DIM], keys(lo, n))
        s_ref[h % 2, :, lo:lo + n] = s
        for k in range(n // HEAD_DIM):
            blk = s[:, k * HEAD_DIM:(k + 1) * HEAD_DIM]
            m_lanes = blk if m_lanes is None else jnp.maximum(m_lanes, blk)
        return m_lanes

    def weighted(h, lo, n, m, o_ext):
        p = jnp.exp2(s_ref[h % 2, :, lo:lo + n] - m).astype(BF16)
        part = _dot(p, ve_ref[lo:lo + n, :])
        return part if o_ext is None else o_ext + part

    def finish(h, o_ext):
        o_ref[:, h * HEAD_DIM:(h + 1) * HEAD_DIM] = (o_ext[:, :HEAD_DIM] / o_ext[:, HEAD_DIM:]).astype(BF16)

    def run(chunk_list):
        m_lanes = None
        for lo, n in chunk_list:
            m_lanes = scores(0, lo, n, m_lanes)
        for h in range(Q_PER_KV):
            m = jnp.max(m_lanes, axis=-1, keepdims=True)
            m_lanes, o_ext = None, None
            for lo, n in chunk_list:
                o_ext = weighted(h, lo, n, m, o_ext)
                if h + 1 < Q_PER_KV:
                    m_lanes = scores(h + 1, lo, n, m_lanes)
            finish(h, o_ext)

    @pl.when(qi < nq)
    def _():
        run(chunks)

    @pl.when(qi == nq)
    def _():
        run(chunks[:1])


def _attention(qk, proj, *, n_batch, seq_len, ctx_len):
    rows = qk.shape[0]
    nq = seq_len // TQ
    ctx_blk0 = n_batch * seq_len // ctx_len
    kcol = N_Q_HEADS
    vcol = COL_V // HEAD_DIM
    qw = Q_PER_KV * HEAD_DIM

    def q_map(b, g, qi):
        return (jnp.where(qi < nq, b * nq + qi, n_batch * nq + b), g)

    return pl.pallas_call(
        functools.partial(_attn_kernel, nq=nq, ctx_len=ctx_len, seq_len=seq_len),
        grid=(n_batch, N_KV_HEADS, nq + 1),
        in_specs=[
            pl.BlockSpec((TQ, qw), q_map),
            pl.BlockSpec((seq_len, HEAD_DIM), lambda b, g, qi: (b, kcol + g)),
            pl.BlockSpec((ctx_len, HEAD_DIM), lambda b, g, qi: (ctx_blk0 + b, kcol + g)),
            pl.BlockSpec((seq_len, HEAD_DIM), lambda b, g, qi: (b, vcol + g)),
            pl.BlockSpec((ctx_len, HEAD_DIM), lambda b, g, qi: (ctx_blk0 + b, vcol + g)),
        ],
        out_specs=pl.BlockSpec((TQ, qw), q_map),
        out_shape=jax.ShapeDtypeStruct((rows, ATTN_DIM), BF16),
        scratch_shapes=[pltpu.VMEM((2, TQ, ctx_len + seq_len), F32),
                        pltpu.VMEM((ctx_len + seq_len, 2 * HEAD_DIM), BF16)],
        compiler_params=_cparams(("parallel", "parallel", "arbitrary")),
        name="attention",
    )(qk, qk, qk, proj, proj)


def _ssd_expand_table():
    er = jnp.arange(2 * DT_W)[:, None] % DT_W
    ec = jnp.arange(3 * SSD_DIM)[None, :]
    return (er == (ec // SSD_DIM) * SSD_HEADS + (ec % SSD_DIM) // SSD_HEAD_DIM).astype(BF16)


def _ssd_chunk(fwd, x_ref, b_ref, c_ref, dt_ref, dtt_ref, bias_ref, biast_ref, alog_ref, alogt_ref,
               expand_ref, y_ref, st_ref):
    nh = SSD_HEADS
    gw = HEADS_PER_GROUP * SSD_HEAD_DIM
    n = SSD_CHUNK

    row = lax.broadcasted_iota(jnp.int32, (n, n), 0)
    col = lax.broadcasted_iota(jnp.int32, (n, n), 1)
    mask = (col <= row) if fwd else (col >= row)
    tri = mask.astype(BF16)
    tri_t = ((row <= col) if fwd else (row >= col)).astype(BF16)

    dt_raw = dt_ref[...] if fwd else pltpu.roll(dt_ref[...], DT_W - nh, 1)
    dtv = _softplus(dt_raw + bias_ref[...])
    da = dtv * (-jnp.exp(alog_ref[...]))
    da3 = _split3(da)
    cs = _dot(tri, da3[0]) + _dot(tri, da3[1]) + _dot(tri, da3[2])
    total = cs[n - 1:n, :] if fwd else cs[0:1, :]

    dtv_t = _softplus(dtt_ref[...] + biast_ref[...])
    da_t = dtv_t * (-jnp.exp(alogt_ref[...]))
    da_t3 = _split3(da_t)
    cs_t = _dot(da_t3[0], tri_t) + _dot(da_t3[1], tri_t) + _dot(da_t3[2], tri_t)

    decay_end = jnp.exp(total - cs)
    ecs = jnp.exp(cs)
    lane = lax.broadcasted_iota(jnp.int32, (n, DT_W), 1)
    packed = jnp.where(lane < nh, dtv,
                       jnp.where(lane < 2 * nh, pltpu.roll(dtv * decay_end, nh, 1),
                                 jnp.where(lane < 3 * nh, pltpu.roll(ecs, 2 * nh, 1), 0.0)))
    ex = _dot(jnp.concatenate(_split2(packed), axis=1), expand_ref[...])
    dt_x = ex[:, 0:SSD_DIM]
    dtdec_x = ex[:, SSD_DIM:2 * SSD_DIM]
    ecs_x = ex[:, 2 * SSD_DIM:3 * SSD_DIM]

    xf = x_ref[...].astype(F32)
    xdt = (xf * dt_x).astype(BF16)
    xdec = (xf * dtdec_x).astype(BF16)
    lane_h = lax.broadcasted_iota(jnp.int32, (n, 2 * SSD_HEAD_DIM), 1)

    for g in range(SSD_GROUPS):
        bg = b_ref[:, g * SSD_STATE:(g + 1) * SSD_STATE]
        cg = c_ref[:, g * SSD_STATE:(g + 1) * SSD_STATE]
        cb = _dot_nt(cg, bg)
        st = st_ref[g]
        y_off = _dot(cg, st.astype(BF16)) * ecs_x[:, g * gw:(g + 1) * gw]
        for j in range(HEADS_PER_GROUP // 2):
            ws = []
            for e in (2 * j, 2 * j + 1):
                hcol = g * HEADS_PER_GROUP + e
                diff = cs[:, hcol:hcol + 1] - cs_t[hcol:hcol + 1, :]
                lm = jnp.exp(jnp.where(mask, diff, -1e30))
                ws.append((cb * lm).astype(BF16))
            w_pair = jnp.concatenate(ws, axis=1)
            lo = g * gw + j * 2 * SSD_HEAD_DIM
            x2 = xdt[:, lo:lo + 2 * SSD_HEAD_DIM]
            rhs = jnp.concatenate([jnp.where(lane_h < SSD_HEAD_DIM, x2, jnp.zeros_like(x2)),
                                   jnp.where(lane_h >= SSD_HEAD_DIM, x2, jnp.zeros_like(x2))], axis=0)
            y_ref[:, lo:lo + 2 * SSD_HEAD_DIM] = (
                _dot(w_pair, rhs) + y_off[:, j * 2 * SSD_HEAD_DIM:(j + 1) * 2 * SSD_HEAD_DIM])
        new_states = _dot_tn(bg, xdec[:, g * gw:(g + 1) * gw])
        etot = ecs_x[n - 1:n, g * gw:(g + 1) * gw] if fwd else ecs_x[0:1, g * gw:(g + 1) * gw]
        st_ref[g] = st * etot + new_states


def _ssd_kernel(*refs):
    n_in = 9
    expand_ref = refs[2 * n_in]
    st_ref = refs[-1]

    @pl.when(pl.program_id(1) == 0)
    def _():
        st_ref[...] = jnp.zeros_like(st_ref)

    for d in range(2):
        _ssd_chunk(d == 0, *refs[d * n_in:(d + 1) * n_in], expand_ref, refs[2 * n_in + 1 + d], st_ref.at[d])


def _ssd(xbc, dt, dt_t, bias, bias_t, alog, alog_t, *, layer, n_batch, seq_len, ctx_len):
    rows = xbc.shape[0]
    ncc = ctx_len // SSD_CHUNK
    ncl = seq_len // SSD_CHUNK
    lat_blk = seq_len // SSD_CHUNK
    ctx_blk0 = n_batch * seq_len // SSD_CHUNK
    bcol = SSD_DIM // (SSD_GROUPS * SSD_STATE)

    def rb(d):
        def block(b, c):
            cc = c if d == 0 else ncc - 1 - c
            lc = c - ncc if d == 0 else ncl - 1 - (c - ncc)
            return jnp.where(c < ncc, ctx_blk0 + b * ncc + cc, b * lat_blk + lc)
        return block

    def operand_specs(d):
        blk = rb(d)
        return [
            pl.BlockSpec((SSD_CHUNK, SSD_DIM), lambda b, c: (blk(b, c), 0)),
            pl.BlockSpec((SSD_CHUNK, SSD_GROUPS * SSD_STATE), lambda b, c: (blk(b, c), bcol)),
            pl.BlockSpec((SSD_CHUNK, SSD_GROUPS * SSD_STATE), lambda b, c: (blk(b, c), bcol + 1)),
            pl.BlockSpec((SSD_CHUNK, DT_W), lambda b, c: (blk(b, c), 0)),
            pl.BlockSpec((SSD_HEADS, SSD_CHUNK), lambda b, c: (d, blk(b, c))),
            pl.BlockSpec((None, None, 1, DT_W), lambda b, c: (layer, d, 0, 0)),
            pl.BlockSpec((None, None, SSD_HEADS, 1), lambda b, c: (layer, d, 0, 0)),
            pl.BlockSpec((None, None, 1, DT_W), lambda b, c: (layer, d, 0, 0)),
            pl.BlockSpec((None, None, SSD_HEADS, 1), lambda b, c: (layer, d, 0, 0)),
        ]

    operands = (xbc, xbc, xbc, dt, dt_t, bias, bias_t, alog, alog_t)
    y_shape = jax.ShapeDtypeStruct((rows, SSD_DIM), F32)
    return pl.pallas_call(
        _ssd_kernel,
        grid=(n_batch, ncc + ncl),
        in_specs=operand_specs(0) + operand_specs(1) + [
            pl.BlockSpec((2 * DT_W, 3 * SSD_DIM), lambda b, c: (0, 0))],
        out_specs=[pl.BlockSpec((SSD_CHUNK, SSD_DIM), lambda b, c: (rb(0)(b, c), 0)),
                   pl.BlockSpec((SSD_CHUNK, SSD_DIM), lambda b, c: (rb(1)(b, c), 0))],
        out_shape=[y_shape, y_shape],
        scratch_shapes=[pltpu.VMEM((2, SSD_GROUPS, SSD_STATE, HEADS_PER_GROUP * SSD_HEAD_DIM), F32)],
        compiler_params=_cparams(("parallel", "arbitrary")),
        name="ssd",
    )(*operands, *operands, _ssd_expand_table())


def _fourier_fold(u_refs, chan_ref, p_ref, ah_ref, n):
    gd = FOURIER_GROUP_DIM
    half = n // 2
    bs = min(256, half)
    nb = half // bs
    r_i = lax.broadcasted_iota(jnp.int32, (bs, bs), 0)
    c_i = lax.broadcasted_iota(jnp.int32, (bs, bs), 1)
    flip = (r_i + c_i == bs).astype(BF16)
    row0 = lax.broadcasted_iota(jnp.int32, (bs, gd), 0) == 0
    cos_c = chan_ref[:, :gd]
    sin_c = chan_ref[:, gd:]
    for g, u_ref in enumerate(u_refs):
        cols = slice(g * gd, (g + 1) * gd)
        for i in range(nb):
            lo = u_ref[bs * i:bs * (i + 1), :].astype(F32)
            rev = _dot(flip, u_ref[bs * (2 * nb - 1 - i):bs * (2 * nb - i), :])
            if i > 0:
                first = u_ref[bs * (2 * nb - i):bs * (2 * nb - i) + 16, :][0:1, :].astype(F32)
                rev = jnp.where(row0, first, rev)
            p_ref[bs * i:bs * (i + 1), cols] = _dot((lo + rev).astype(BF16), cos_c).astype(BF16)
            p_ref[half + bs * i:half + bs * (i + 1), cols] = _dot((lo - rev).astype(BF16), sin_c).astype(BF16)
        ah_ref[:, cols] = _dot(u_ref[half:half + 16, :], cos_c)


def _fourier_rows(m_ref, p_ref, ah_ref, o_ref, n):
    rows = m_ref.shape[0]
    parity = lax.broadcasted_iota(jnp.int32, (rows, 1), 0) & 1
    sign = (1 - 2 * parity).astype(F32)
    nyquist = ah_ref[0:1, :] * (n ** -0.5)
    o_ref[...] = (_dot(m_ref[...], p_ref[0:n, :]) + sign * nyquist).astype(BF16)


def _fourier_kernel(ul0, ul1, ul2, ul3, uc0, uc1, uc2, uc3, chan_ref, ml_ref, mc_ref, o_ref, p_ref, ah_ref, *,
                    nm, seq_len, ctx_len):
    mi = pl.program_id(1)

    @pl.when(mi == 0)
    def _():
        _fourier_fold((ul0, ul1, ul2, ul3), chan_ref, p_ref, ah_ref, seq_len)

    @pl.when(mi < nm)
    def _():
        _fourier_rows(ml_ref, p_ref, ah_ref, o_ref, seq_len)

    @pl.when(mi == nm)
    def _():
        _fourier_fold((uc0, uc1, uc2, uc3), chan_ref, p_ref, ah_ref, ctx_len)
        _fourier_rows(mc_ref, p_ref, ah_ref, o_ref, ctx_len)


def _fourier(proj, chan_mat, seq_mat_l, seq_mat_c, *, n_batch, seq_len, ctx_len):
    rows = proj.shape[0]
    tm = TM_FOURIER
    nm = seq_len // tm
    gd = FOURIER_GROUP_DIM
    col0 = COL_U // gd
    ctx_blk0 = n_batch * seq_len // ctx_len

    def u_specs(block_rows, blk0):
        return [pl.BlockSpec((block_rows, gd), functools.partial(lambda b, mi, g: (blk0 + b, col0 + g), g=g))
                for g in range(FOURIER_GROUPS)]

    def out_map(b, mi):
        return (jnp.where(mi < nm, b * nm + mi, n_batch * nm + b), 0)

    return pl.pallas_call(
        functools.partial(_fourier_kernel, nm=nm, seq_len=seq_len, ctx_len=ctx_len),
        grid=(n_batch, nm + 1),
        in_specs=u_specs(seq_len, 0) + u_specs(ctx_len, ctx_blk0) + [
            pl.BlockSpec((gd, 2 * gd), lambda b, mi: (0, 0)),
            pl.BlockSpec((tm, seq_len), lambda b, mi: (jnp.minimum(mi, nm - 1), 0)),
            pl.BlockSpec((ctx_len, ctx_len), lambda b, mi: (0, 0)),
        ],
        out_specs=pl.BlockSpec((tm, FOURIER_DIM), out_map),
        out_shape=jax.ShapeDtypeStruct((rows, FOURIER_DIM), BF16),
        scratch_shapes=[pltpu.VMEM((seq_len, FOURIER_DIM), BF16), pltpu.VMEM((16, FOURIER_DIM), F32)],
        compiler_params=_cparams(("parallel", "arbitrary")),
        name="fourier",
    )(*([proj] * (2 * FOURIER_GROUPS)), chan_mat, seq_mat_l, seq_mat_c)


def _dft_cos_sin(n, n_cols):
    j = jnp.arange(n, dtype=jnp.int32)
    jk = (j[:, None] * j[None, :n_cols]) % n
    ang = jk.astype(F32) * (2.0 * math.pi / n)
    return jnp.cos(ang), jnp.sin(ang)


def _dft_tables(n):
    cos_m, sin_m = _dft_cos_sin(n, n // 2)
    return (jnp.concatenate([cos_m, -sin_m], axis=1) * (n ** -0.5)).astype(BF16)


def _chan_table():
    cos_m, sin_m = _dft_cos_sin(FOURIER_GROUP_DIM, FOURIER_GROUP_DIM)
    return (jnp.concatenate([cos_m, sin_m], axis=1) * (FOURIER_GROUP_DIM ** -0.5)).astype(BF16)


def _merge_kernel(x_ref, mod_ref, yf_ref, yb_ref, xh_ref, z_ref, four_ref, attn_ref, g_ref, wb_ref, wo_ref,
                  dsk_ref, ng_ref, lg_ref, lb_ref, o_ref):
    y = yf_ref[...] + yb_ref[...] + dsk_ref[...] * xh_ref[...].astype(F32)
    y = y * _silu(z_ref[...].astype(F32))
    gw = SSD_DIM // SSD_GROUPS
    heads = []
    for g in range(SSD_GROUPS):
        yg = y[:, g * gw:(g + 1) * gw]
        yg = yg * lax.rsqrt(jnp.mean(yg * yg, axis=-1, keepdims=True) + RMS_EPS)
        heads.append((yg * ng_ref[:, g * gw:(g + 1) * gw]).astype(BF16))
    branches = (jnp.concatenate(heads, axis=1), four_ref[...], attn_ref[...])

    acc = None
    for j in range(D_MODEL // TN_MERGE):
        lo = j * TN_MERGE
        m = None
        for k, yk in enumerate(branches):
            gate = jax.nn.sigmoid(g_ref[:, k * D_MODEL + lo:k * D_MODEL + lo + TN_MERGE].astype(F32))
            term = gate * _dot(yk, wb_ref[k, :, lo:lo + TN_MERGE])
            m = term if m is None else m + term
        part = _dot(m.astype(BF16), wo_ref[lo:lo + TN_MERGE, :])
        acc = part if acc is None else acc + part

    v = ALPHA * x_ref[...] + mod_ref[5:6, :] * acc
    o_ref[...] = _layer_norm(v, lg_ref[...], lb_ref[...])


def _merge(xs, rows, mods, y_ssd, xbc, proj, four, attn, w_branch, w_out, dsk, norm_g, ln_g, ln_b, *,
           layer, seq_len, n_batch):
    tiles_per_batch = seq_len // TM_MERGE
    resident = pl.Buffered(1)
    row_spec = pl.BlockSpec((TM_MERGE, BRANCH_DIM), lambda i: (i, 0))
    vec_spec = pl.BlockSpec((None, 1, BRANCH_DIM), lambda i: (layer, 0, 0))
    ln_spec = pl.BlockSpec((None, None, 1, D_MODEL), lambda i: (layer, 1, 0, 0))
    return pl.pallas_call(
        _merge_kernel,
        grid=(rows // TM_MERGE,),
        in_specs=[
            pl.BlockSpec((TM_MERGE, D_MODEL), lambda i: (i, 0)),
            pl.BlockSpec((None, None, N_MOD, D_MODEL),
                         lambda i: (layer, jnp.minimum(i // tiles_per_batch, n_batch), 0, 0)),
            row_spec,
            row_spec,
            row_spec,
            pl.BlockSpec((TM_MERGE, BRANCH_DIM), lambda i: (i, COL_Z // BRANCH_DIM)),
            row_spec,
            row_spec,
            pl.BlockSpec((TM_MERGE, N_BRANCH * D_MODEL), lambda i: (i, COL_G // (N_BRANCH * D_MODEL))),
            pl.BlockSpec((None, N_BRANCH, BRANCH_DIM, D_MODEL), lambda i: (layer, 0, 0, 0), pipeline_mode=resident),
            pl.BlockSpec((None, D_MODEL, D_MODEL), lambda i: (layer, 0, 0), pipeline_mode=resident),
            vec_spec, vec_spec,
            ln_spec, ln_spec,
        ],
        out_specs=pl.BlockSpec((TM_MERGE, D_MODEL), lambda i: (i, 0)),
        out_shape=jax.ShapeDtypeStruct((rows, D_MODEL), F32),
        compiler_params=_cparams(("parallel",)),
        name="merge",
    )(xs, mods, y_ssd[0], y_ssd[1], xbc, proj, four, attn, proj, w_branch, w_out, dsk, norm_g, ln_g, ln_b)


def kernel(x, c, ctx, c_ctx, w_ada, b_ada, ln_g, ln_b, ffn_wgu, ffn_wd, w_in, conv_w, conv_b, dt_bias, a_log,
           d_skip, ssd_norm_g, q_norm_g, k_norm_g, w_branch, w_out):
    n_batch, seq_len, _ = x.shape
    ctx_len = ctx.shape[1]
    lat_rows = n_batch * seq_len
    rows = lat_rows + n_batch * ctx_len
    assert ctx_len == TQ and seq_len % TM == 0 and (n_batch * ctx_len) % TM == 0 and n_batch < MOD_ROWS
    tiles_per_batch = seq_len // TM
    tile_kw = dict(tiles_per_batch=tiles_per_batch, n_batch=n_batch)

    wgu_bf = ffn_wgu.astype(BF16)
    wd_bf = ffn_wd.astype(BF16)
    wb_bf = w_branch.astype(BF16)
    wo_bf = w_out.astype(BF16)
    w_z, w_xbc, w_dtf, w_dtb, w_u, w_q, w_k, w_v, w_g = jnp.split(w_in, IN_SPLITS, axis=-1)
    w_main = jnp.concatenate([w_g, w_z, w_u, w_q, w_xbc, w_k, w_v], axis=-1).astype(BF16)
    w_dt = jnp.concatenate([w_dtf, w_dtb, jnp.zeros((DEPTH, D_MODEL, DT_W - 2 * SSD_HEADS), F32)],
                           axis=-1).astype(BF16)
    head_pad = ((0, 0), (0, 0), (0, DT_W - SSD_HEADS))
    bias_row = jnp.pad(dt_bias, head_pad).reshape(DEPTH, 2, 1, DT_W)
    alog_row = jnp.pad(a_log, head_pad).reshape(DEPTH, 2, 1, DT_W)
    bias_col = dt_bias.reshape(DEPTH, 2, SSD_HEADS, 1)
    alog_col = a_log.reshape(DEPTH, 2, SSD_HEADS, 1)
    dsk_all = jnp.repeat(d_skip, SSD_HEAD_DIM, axis=-1).reshape(DEPTH, 1, SSD_DIM)
    norm_g_all = ssd_norm_g.reshape(DEPTH, 1, SSD_DIM)
    qk_gains = jnp.stack([q_norm_g, k_norm_g], axis=1)
    ln_g4 = ln_g.reshape(DEPTH, 3, 1, D_MODEL)
    ln_b4 = ln_b.reshape(DEPTH, 3, 1, D_MODEL)
    cos_t, sin_t = _rope_tables(seq_len)
    chan_mat = _chan_table()
    seq_mat_l = _dft_tables(seq_len)
    seq_mat_c = _dft_tables(ctx_len)

    cvec = jnp.concatenate([c, c_ctx[None], jnp.zeros((MOD_ROWS - n_batch - 1, D_MODEL), F32)], axis=0)
    mods = _adaln_all(cvec, w_ada, b_ada)

    xs = jnp.concatenate([x.reshape(lat_rows, D_MODEL), ctx.reshape(n_batch * ctx_len, D_MODEL)], axis=0)
    for layer in range(DEPTH):
        last = layer == DEPTH - 1
        xs = _ffn(xs, rows, mods, wgu_bf, wd_bf, ln_g4, ln_b4, layer=layer, which=0, sub=0, **tile_kw)

        proj, dt = _inproj(xs, mods, w_main, w_dt, layer=layer, **tile_kw)
        xbc = _conv(proj, conv_w, conv_b, layer=layer, lat_rows=lat_rows, seq_len=seq_len)
        y_ssd = _ssd(xbc, dt, dt.T, bias_row, bias_col, alog_row, alog_col,
                     layer=layer, n_batch=n_batch, seq_len=seq_len, ctx_len=ctx_len)
        qk = _qk_prep(proj, qk_gains[layer], cos_t, sin_t, lat_rows=lat_rows, seq_len=seq_len)
        attn = _attention(qk, proj, n_batch=n_batch, seq_len=seq_len, ctx_len=ctx_len)
        four = _fourier(proj, chan_mat, seq_mat_l, seq_mat_c, n_batch=n_batch, seq_len=seq_len, ctx_len=ctx_len)
        out_rows = lat_rows if last else rows
        xs = _merge(xs, out_rows, mods, y_ssd, xbc, proj, four, attn, wb_bf, wo_bf, dsk_all, norm_g_all,
                    ln_g4, ln_b4, layer=layer, seq_len=seq_len, n_batch=n_batch)
        xs = _ffn(xs, out_rows, mods, wgu_bf, wd_bf, ln_g4, ln_b4, layer=layer, which=1, sub=2, **tile_kw)
    return xs.reshape(n_batch, seq_len, D_MODEL)
```

```python
import functools
import math

import jax
import jax.numpy as jnp
from jax import lax
from jax.experimental import pallas as pl
from jax.experimental.pallas import tpu as pltpu

F32 = jnp.float32
BF16 = jnp.bfloat16

D_MODEL = 2048
DEPTH = 4
GRID_W = 64
N_MOD = 9
ALPHA = (2 * DEPTH) ** 0.25
LN_EPS = 1e-6
RMS_EPS = 1e-6
D_FF = 5632
BRANCH_DIM = D_MODEL // 2
N_BRANCH = 3
SSD_DIM = BRANCH_DIM
SSD_HEAD_DIM = 64
SSD_HEADS = SSD_DIM // SSD_HEAD_DIM
SSD_GROUPS = 2
HEADS_PER_GROUP = SSD_HEADS // SSD_GROUPS
SSD_STATE = 128
SSD_CHUNK = 128
D_CONV = 5
CONV_DIM = SSD_DIM + 2 * SSD_GROUPS * SSD_STATE
FOURIER_DIM = BRANCH_DIM
FOURIER_GROUPS = 4
FOURIER_GROUP_DIM = FOURIER_DIM // FOURIER_GROUPS
HEAD_DIM = 128
N_Q_HEADS = BRANCH_DIM // HEAD_DIM
N_KV_HEADS = 2
Q_PER_KV = N_Q_HEADS // N_KV_HEADS
ATTN_DIM = N_Q_HEADS * HEAD_DIM
KV_DIM = N_KV_HEADS * HEAD_DIM
AXIS_ROPE_DIM = HEAD_DIM // 2
ROPE_THETA = 10000.0
IN_WIDTHS = (SSD_DIM, CONV_DIM, SSD_HEADS, SSD_HEADS, FOURIER_DIM, ATTN_DIM, KV_DIM, KV_DIM, N_BRANCH * D_MODEL)
IN_SPLITS = tuple(sum(IN_WIDTHS[:i + 1]) for i in range(len(IN_WIDTHS) - 1))

COL_G = 0
COL_Z = COL_G + N_BRANCH * D_MODEL
COL_U = COL_Z + SSD_DIM
COL_Q = COL_U + FOURIER_DIM
COL_XBC = COL_Q + ATTN_DIM
COL_K = COL_XBC + CONV_DIM
COL_V = COL_K + KV_DIM
PROJ_W = COL_V + KV_DIM
DT_W = 128
MOD_ROWS = 16

TM = 512
TF = 512
TN_PROJ = 2816
FFN_SPLIT = 2
TM_MERGE = 256
TN_MERGE = 512
TN_ADA = 1024
CONV_ROWS = 256
CONV_HALO = 16
TQ = 256
TK = 512
TM_FOURIER = 256
VMEM_LIMIT = 56 * 1024 * 1024


def _cparams(sem):
    return pltpu.CompilerParams(dimension_semantics=sem, vmem_limit_bytes=VMEM_LIMIT)


def _silu(x):
    return x * jax.nn.sigmoid(x)


def _softplus(x):
    return jnp.maximum(x, 0.0) + jnp.log1p(jnp.exp(-jnp.abs(x)))


def _layer_norm(v, g, b):
    mu = jnp.mean(v, axis=-1, keepdims=True)
    d = v - mu
    var = jnp.mean(d * d, axis=-1, keepdims=True)
    return d * lax.rsqrt(var + LN_EPS) * g + b


def _dot(a, b):
    return jnp.dot(a, b, preferred_element_type=F32)


def _dot_nt(a, b):
    return lax.dot_general(a, b, (((1,), (1,)), ((), ())), preferred_element_type=F32)


def _dot_tn(a, b):
    return lax.dot_general(a, b, (((0,), (0,)), ((), ())), preferred_element_type=F32)


def _split2(a):
    hi = a.astype(BF16)
    lo = (a - hi.astype(F32)).astype(BF16)
    return hi, lo


def _split3(a):
    hi = a.astype(BF16)
    r = a - hi.astype(F32)
    mid = r.astype(BF16)
    lo = (r - mid.astype(F32)).astype(BF16)
    return hi, mid, lo


def _ada_kernel(c_ref, w_ref, b_ref, o_ref):
    h = _silu(c_ref[...]).astype(BF16)
    o_ref[...] = _dot(h, w_ref[...].astype(BF16)) + b_ref[...]


def _adaln_all(cvec, w_ada, b_ada):
    n = N_MOD * D_MODEL
    out = pl.pallas_call(
        _ada_kernel,
        grid=(DEPTH, n // TN_ADA),
        in_specs=[
            pl.BlockSpec((MOD_ROWS, D_MODEL), lambda l, j: (0, 0)),
            pl.BlockSpec((None, D_MODEL, TN_ADA), lambda l, j: (l, 0, j)),
            pl.BlockSpec((None, 1, TN_ADA), lambda l, j: (l, 0, j)),
        ],
        out_specs=pl.BlockSpec((None, MOD_ROWS, TN_ADA), lambda l, j: (l, 0, j)),
        out_shape=jax.ShapeDtypeStruct((DEPTH, MOD_ROWS, n), F32),
        compiler_params=_cparams(("parallel", "parallel")),
        name="adaln",
    )(cvec, w_ada, b_ada.reshape(DEPTH, 1, n))
    return out.reshape(DEPTH, MOD_ROWS, N_MOD, D_MODEL)


def _mod_spec(layer, tiles_per_batch, n_batch):
    return pl.BlockSpec((None, None, N_MOD, D_MODEL),
                        lambda i, j: (layer, jnp.minimum(i // tiles_per_batch, n_batch), 0, 0))


def _ln_spec(layer, sub):
    return pl.BlockSpec((None, None, 1, D_MODEL), lambda i, j: (layer, sub, 0, 0))


def _ffn_kernel(x_ref, mod_ref, wg_ref, wu_ref, wd_ref, g_ref, b_ref, o_ref, h_ref, *, sub, nj):
    j = pl.program_id(1)
    cols = TF // FFN_SPLIT

    def step(first, last):
        if first:
            shift = mod_ref[3 * sub:3 * sub + 1, :]
            scale = mod_ref[3 * sub + 1:3 * sub + 2, :]
            h = (x_ref[...] * (1.0 + scale) + shift).astype(BF16)
            h_ref[...] = h
        else:
            h = h_ref[...]
        acc = None
        for c in range(FFN_SPLIT):
            cs = slice(c * cols, (c + 1) * cols)
            act = (_silu(_dot(h, wg_ref[:, cs])) * _dot(h, wu_ref[:, cs])).astype(BF16)
            part = _dot(act, wd_ref[cs, :])
            acc = part if acc is None else acc + part
        if not first:
            acc = o_ref[...] + acc
        if last:
            gain = mod_ref[3 * sub + 2:3 * sub + 3, :]
            acc = _layer_norm(ALPHA * x_ref[...] + 0.5 * gain * acc, g_ref[...], b_ref[...])
        o_ref[...] = acc

    @pl.when(j == 0)
    def _():
        step(True, False)

    @pl.when(jnp.logical_and(j > 0, j < nj - 1))
    def _():
        step(False, False)

    @pl.when(j == nj - 1)
    def _():
        step(False, True)


def _ffn(xs, rows, mods, wgu, wd, ln_g, ln_b, *, layer, which, sub, tiles_per_batch, n_batch):
    nj = D_FF // TF
    return pl.pallas_call(
        functools.partial(_ffn_kernel, sub=sub, nj=nj),
        grid=(rows // TM, nj),
        in_specs=[
            pl.BlockSpec((TM, D_MODEL), lambda i, j: (i, 0)),
            _mod_spec(layer, tiles_per_batch, n_batch),
            pl.BlockSpec((None, None, D_MODEL, TF), lambda i, j: (layer, which, 0, j)),
            pl.BlockSpec((None, None, D_MODEL, TF), lambda i, j: (layer, which, 0, nj + j)),
            pl.BlockSpec((None, None, TF, D_MODEL), lambda i, j: (layer, which, j, 0)),
            _ln_spec(layer, sub),
            _ln_spec(layer, sub),
        ],
        out_specs=pl.BlockSpec((TM, D_MODEL), lambda i, j: (i, 0)),
        out_shape=jax.ShapeDtypeStruct((rows, D_MODEL), F32),
        scratch_shapes=[pltpu.VMEM((TM, D_MODEL), BF16)],
        compiler_params=_cparams(("parallel", "arbitrary")),
        name="ffn",
    )(xs, mods, wgu, wgu, wd, ln_g, ln_b)


def _inproj_kernel(x_ref, mod_ref, w_ref, wdt_ref, o_ref, dt_ref, h_ref):
    j = pl.program_id(1)

    @pl.when(j == 0)
    def _():
        shift = mod_ref[3:4, :]
        scale = mod_ref[4:5, :]
        h = (x_ref[...] * (1.0 + scale) + shift).astype(BF16)
        h_ref[...] = h
        dt_ref[...] = _dot(h, wdt_ref[...])

    o_ref[...] = _dot(h_ref[...], w_ref[...]).astype(BF16)


def _inproj(xs, mods, w_main, w_dt, *, layer, tiles_per_batch, n_batch):
    rows = xs.shape[0]
    return pl.pallas_call(
        _inproj_kernel,
        grid=(rows // TM, PROJ_W // TN_PROJ),
        in_specs=[
            pl.BlockSpec((TM, D_MODEL), lambda i, j: (i, 0)),
            _mod_spec(layer, tiles_per_batch, n_batch),
            pl.BlockSpec((None, D_MODEL, TN_PROJ), lambda i, j: (layer, 0, j)),
            pl.BlockSpec((None, D_MODEL, DT_W), lambda i, j: (layer, 0, 0)),
        ],
        out_specs=[
            pl.BlockSpec((TM, TN_PROJ), lambda i, j: (i, j)),
            pl.BlockSpec((TM, DT_W), lambda i, j: (i, 0)),
        ],
        out_shape=[
            jax.ShapeDtypeStruct((rows, PROJ_W), BF16),
            jax.ShapeDtypeStruct((rows, DT_W), F32),
        ],
        scratch_shapes=[pltpu.VMEM((TM, D_MODEL), BF16)],
        compiler_params=_cparams(("parallel", "arbitrary")),
        name="inproj",
    )(xs, mods, w_main, w_dt)


def _conv_kernel(prev_ref, cur_ref, next_ref, w_ref, b_ref, o_ref, *, lat_tiles, tiles_per_seq):
    k = pl.program_id(0)
    in_ctx = k >= lat_tiles
    pos = k % tiles_per_seq
    first = jnp.logical_or(in_ctx, pos == 0)
    last = jnp.logical_or(in_ctx, pos == tiles_per_seq - 1)
    cur = cur_ref[...].astype(F32)
    n = cur.shape[0]
    prev = jnp.where(first, 0.0, prev_ref[...].astype(F32))
    nxt = jnp.where(last, 0.0, next_ref[...].astype(F32))
    row = lax.broadcasted_iota(jnp.int32, cur.shape, 0)
    w = w_ref[...]
    acc = cur * w[2:3, :] + b_ref[...]
    s = jnp.where(row == 0, prev[CONV_HALO - 1:CONV_HALO, :], pltpu.roll(cur, 1, 0))
    acc += s * w[1:2, :]
    s = jnp.where(row == 0, prev[CONV_HALO - 2:CONV_HALO - 1, :],
                  jnp.where(row == 1, prev[CONV_HALO - 1:CONV_HALO, :], pltpu.roll(cur, 2, 0)))
    acc += s * w[0:1, :]
    s = jnp.where(row == n - 1, nxt[0:1, :], pltpu.roll(cur, n - 1, 0))
    acc += s * w[3:4, :]
    s = jnp.where(row == n - 1, nxt[1:2, :],
                  jnp.where(row == n - 2, nxt[0:1, :], pltpu.roll(cur, n - 2, 0)))
    acc += s * w[4:5, :]
    o_ref[...] = _silu(acc).astype(BF16)


def _conv(proj, conv_w, conv_b, *, layer, lat_rows, seq_len):
    rows = proj.shape[0]
    halo_per_tile = CONV_ROWS // CONV_HALO
    n_halo = rows // CONV_HALO
    col0 = COL_XBC // CONV_DIM
    return pl.pallas_call(
        functools.partial(_conv_kernel, lat_tiles=lat_rows // CONV_ROWS, tiles_per_seq=seq_len // CONV_ROWS),
        grid=(rows // CONV_ROWS,),
        in_specs=[
            pl.BlockSpec((CONV_HALO, CONV_DIM), lambda k: (jnp.maximum(k * halo_per_tile - 1, 0), col0)),
            pl.BlockSpec((CONV_ROWS, CONV_DIM), lambda k: (k, col0)),
            pl.BlockSpec((CONV_HALO, CONV_DIM), lambda k: (jnp.minimum((k + 1) * halo_per_tile, n_halo - 1), col0)),
            pl.BlockSpec((None, D_CONV, CONV_DIM), lambda k: (layer, 0, 0)),
            pl.BlockSpec((None, 1, CONV_DIM), lambda k: (layer, 0, 0)),
        ],
        out_specs=pl.BlockSpec((CONV_ROWS, CONV_DIM), lambda k: (k, 0)),
        out_shape=jax.ShapeDtypeStruct((rows, CONV_DIM), BF16),
        compiler_params=_cparams(("parallel",)),
        name="conv",
    )(proj, proj, proj, conv_w, conv_b.reshape(DEPTH, 1, CONV_DIM))


def _qk_kernel(q_ref, k_ref, g_ref, cos_ref, sin_ref, o_ref, *, lat_tiles):
    is_lat = pl.program_id(0) < lat_tiles
    cos = jnp.where(is_lat, cos_ref[...], 1.0)
    sin = jnp.where(is_lat, sin_ref[...], 0.0)
    lane = lax.broadcasted_iota(jnp.int32, cos.shape, 1)
    half = AXIS_ROPE_DIM // 2
    first_half = (lane % AXIS_ROPE_DIM) < half
    q_scale = HEAD_DIM ** -0.5 * math.log2(math.e)
    for h in range(N_Q_HEADS + N_KV_HEADS):
        if h < N_Q_HEADS:
            x = q_ref[:, h * HEAD_DIM:(h + 1) * HEAD_DIM].astype(F32)
            gain = g_ref[0:1, :] * q_scale
        else:
            x = k_ref[:, (h - N_Q_HEADS) * HEAD_DIM:(h - N_Q_HEADS + 1) * HEAD_DIM].astype(F32)
            gain = g_ref[1:2, :]
        ssum = _dot(jnp.concatenate(_split3(x * x), axis=1), jnp.ones((3 * HEAD_DIM, HEAD_DIM), BF16))
        xn = x * lax.rsqrt(ssum * (1.0 / HEAD_DIM) + RMS_EPS) * gain
        partner = jnp.where(first_half, pltpu.roll(xn, HEAD_DIM - half, 1), pltpu.roll(xn, half, 1))
        o_ref[:, h * HEAD_DIM:(h + 1) * HEAD_DIM] = (xn * cos + partner * sin).astype(BF16)


def _qk_prep(proj, gains, cos_t, sin_t, *, lat_rows, seq_len):
    rows = proj.shape[0]
    width = ATTN_DIM + KV_DIM
    tiles_per_seq = seq_len // TM
    return pl.pallas_call(
        functools.partial(_qk_kernel, lat_tiles=lat_rows // TM),
        grid=(rows // TM,),
        in_specs=[
            pl.BlockSpec((TM, ATTN_DIM), lambda i: (i, COL_Q // ATTN_DIM)),
            pl.BlockSpec((TM, KV_DIM), lambda i: (i, COL_K // KV_DIM)),
            pl.BlockSpec((2, HEAD_DIM), lambda i: (0, 0)),
            pl.BlockSpec((TM, HEAD_DIM), lambda i: (i % tiles_per_seq, 0)),
            pl.BlockSpec((TM, HEAD_DIM), lambda i: (i % tiles_per_seq, 0)),
        ],
        out_specs=pl.BlockSpec((TM, width), lambda i: (i, 0)),
        out_shape=jax.ShapeDtypeStruct((rows, width), BF16),
        compiler_params=_cparams(("parallel",)),
        name="qk_prep",
    )(proj, proj, gains, cos_t, sin_t)


def _rope_tables(seq_len):
    t = jnp.arange(seq_len)
    row = (t // GRID_W).astype(F32)
    col = (t % GRID_W).astype(F32)
    inv_freq = ROPE_THETA ** (-jnp.arange(0, AXIS_ROPE_DIM, 2, dtype=F32) / AXIS_ROPE_DIM)
    ang_r = row[:, None] * inv_freq
    ang_c = col[:, None] * inv_freq
    cos_t = jnp.concatenate([jnp.cos(ang_r), jnp.cos(ang_r), jnp.cos(ang_c), jnp.cos(ang_c)], axis=-1)
    sin_t = jnp.concatenate([-jnp.sin(ang_r), jnp.sin(ang_r), -jnp.sin(ang_c), jnp.sin(ang_c)], axis=-1)
    return cos_t, sin_t


def _with_ones(v):
    return jnp.concatenate([v, jnp.ones_like(v)], axis=1)


def _attn_kernel(q_ref, kl_ref, kc_ref, vl_ref, vc_ref, o_ref, s_ref, ve_ref, *, nq, ctx_len, seq_len):
    qi = pl.program_id(2)
    chunks = [(0, ctx_len)] + [(ctx_len + c * TK, TK) for c in range(seq_len // TK)]

    @pl.when(qi == 0)
    def _():
        ve_ref[0:ctx_len, :] = _with_ones(vc_ref[...])
        ve_ref[ctx_len:, :] = _with_ones(vl_ref[...])

    def keys(lo, n):
        return kc_ref[...] if lo == 0 else kl_ref[lo - ctx_len:lo - ctx_len + n, :]

    def scores(h, lo, n, m_lanes):
        s = _dot_nt(q_ref[:, h * HEAD_DIM:(h + 1) * HEAD_DIM], keys(lo, n))
        s_ref[h % 2, :, lo:lo + n] = s
        for k in range(n // HEAD_DIM):
            blk = s[:, k * HEAD_DIM:(k + 1) * HEAD_DIM]
            m_lanes = blk if m_lanes is None else jnp.maximum(m_lanes, blk)
        return m_lanes

    def weighted(h, lo, n, m, o_ext):
        p = jnp.exp2(s_ref[h % 2, :, lo:lo + n] - m).astype(BF16)
        part = _dot(p, ve_ref[lo:lo + n, :])
        return part if o_ext is None else o_ext + part

    def finish(h, o_ext):
        o_ref[:, h * HEAD_DIM:(h + 1) * HEAD_DIM] = (o_ext[:, :HEAD_DIM] / o_ext[:, HEAD_DIM:]).astype(BF16)

    def run(chunk_list):
        m_lanes = None
        for lo, n in chunk_list:
            m_lanes = scores(0, lo, n, m_lanes)
        for h in range(Q_PER_KV):
            m = jnp.max(m_lanes, axis=-1, keepdims=True)
            m_lanes, o_ext = None, None
            for lo, n in chunk_list:
                o_ext = weighted(h, lo, n, m, o_ext)
                if h + 1 < Q_PER_KV:
                    m_lanes = scores(h + 1, lo, n, m_lanes)
            finish(h, o_ext)

    @pl.when(qi < nq)
    def _():
        run(chunks)

    @pl.when(qi == nq)
    def _():
        run(chunks[:1])


def _attention(qk, proj, *, n_batch, seq_len, ctx_len):
    rows = qk.shape[0]
    nq = seq_len // TQ
    ctx_blk0 = n_batch * seq_len // ctx_len
    kcol = N_Q_HEADS
    vcol = COL_V // HEAD_DIM
    qw = Q_PER_KV * HEAD_DIM

    def q_map(b, g, qi):
        return (jnp.where(qi < nq, b * nq + qi, n_batch * nq + b), g)

    return pl.pallas_call(
        functools.partial(_attn_kernel, nq=nq, ctx_len=ctx_len, seq_len=seq_len),
        grid=(n_batch, N_KV_HEADS, nq + 1),
        in_specs=[
            pl.BlockSpec((TQ, qw), q_map),
            pl.BlockSpec((seq_len, HEAD_DIM), lambda b, g, qi: (b, kcol + g)),
            pl.BlockSpec((ctx_len, HEAD_DIM), lambda b, g, qi: (ctx_blk0 + b, kcol + g)),
            pl.BlockSpec((seq_len, HEAD_DIM), lambda b, g, qi: (b, vcol + g)),
            pl.BlockSpec((ctx_len, HEAD_DIM), lambda b, g, qi: (ctx_blk0 + b, vcol + g)),
        ],
        out_specs=pl.BlockSpec((TQ, qw), q_map),
        out_shape=jax.ShapeDtypeStruct((rows, ATTN_DIM), BF16),
        scratch_shapes=[pltpu.VMEM((2, TQ, ctx_len + seq_len), F32),
                        pltpu.VMEM((ctx_len + seq_len, 2 * HEAD_DIM), BF16)],
        compiler_params=_cparams(("parallel", "parallel", "arbitrary")),
        name="attention",
    )(qk, qk, qk, proj, proj)


def _ssd_expand_table():
    er = jnp.arange(2 * DT_W)[:, None] % DT_W
    ec = jnp.arange(3 * SSD_DIM)[None, :]
    return (er == (ec // SSD_DIM) * SSD_HEADS + (ec % SSD_DIM) // SSD_HEAD_DIM).astype(BF16)


def _ssd_chunk(fwd, x_ref, b_ref, c_ref, dt_ref, dtt_ref, bias_ref, biast_ref, alog_ref, alogt_ref,
               expand_ref, y_ref, st_ref):
    nh = SSD_HEADS
    gw = HEADS_PER_GROUP * SSD_HEAD_DIM
    n = SSD_CHUNK

    row = lax.broadcasted_iota(jnp.int32, (n, n), 0)
    col = lax.broadcasted_iota(jnp.int32, (n, n), 1)
    mask = (col <= row) if fwd else (col >= row)
    tri = mask.astype(BF16)
    tri_t = ((row <= col) if fwd else (row >= col)).astype(BF16)

    dt_raw = dt_ref[...] if fwd else pltpu.roll(dt_ref[...], DT_W - nh, 1)
    dtv = _softplus(dt_raw + bias_ref[...])
    da = dtv * (-jnp.exp(alog_ref[...]))
    da3 = _split3(da)
    cs = _dot(tri, da3[0]) + _dot(tri, da3[1]) + _dot(tri, da3[2])
    total = cs[n - 1:n, :] if fwd else cs[0:1, :]

    dtv_t = _softplus(dtt_ref[...] + biast_ref[...])
    da_t = dtv_t * (-jnp.exp(alogt_ref[...]))
    da_t3 = _split3(da_t)
    cs_t = _dot(da_t3[0], tri_t) + _dot(da_t3[1], tri_t) + _dot(da_t3[2], tri_t)

    decay_end = jnp.exp(total - cs)
    ecs = jnp.exp(cs)
    lane = lax.broadcasted_iota(jnp.int32, (n, DT_W), 1)
    packed = jnp.where(lane < nh, dtv,
                       jnp.where(lane < 2 * nh, pltpu.roll(dtv * decay_end, nh, 1),
                                 jnp.where(lane < 3 * nh, pltpu.roll(ecs, 2 * nh, 1), 0.0)))
    ex = _dot(jnp.concatenate(_split2(packed), axis=1), expand_ref[...])
    dt_x = ex[:, 0:SSD_DIM]
    dtdec_x = ex[:, SSD_DIM:2 * SSD_DIM]
    ecs_x = ex[:, 2 * SSD_DIM:3 * SSD_DIM]

    xf = x_ref[...].astype(F32)
    xdt = (xf * dt_x).astype(BF16)
    xdec = (xf * dtdec_x).astype(BF16)
    lane_h = lax.broadcasted_iota(jnp.int32, (n, 2 * SSD_HEAD_DIM), 1)

    for g in range(SSD_GROUPS):
        bg = b_ref[:, g * SSD_STATE:(g + 1) * SSD_STATE]
        cg = c_ref[:, g * SSD_STATE:(g + 1) * SSD_STATE]
        cb = _dot_nt(cg, bg)
        st = st_ref[g]
        y_off = _dot(cg, st.astype(BF16)) * ecs_x[:, g * gw:(g + 1) * gw]
        for j in range(HEADS_PER_GROUP // 2):
            ws = []
            for e in (2 * j, 2 * j + 1):
                hcol = g * HEADS_PER_GROUP + e
                diff = cs[:, hcol:hcol + 1] - cs_t[hcol:hcol + 1, :]
                lm = jnp.exp(jnp.where(mask, diff, -1e30))
                ws.append((cb * lm).astype(BF16))
            w_pair = jnp.concatenate(ws, axis=1)
            lo = g * gw + j * 2 * SSD_HEAD_DIM
            x2 = xdt[:, lo:lo + 2 * SSD_HEAD_DIM]
            rhs = jnp.concatenate([jnp.where(lane_h < SSD_HEAD_DIM, x2, jnp.zeros_like(x2)),
                                   jnp.where(lane_h >= SSD_HEAD_DIM, x2, jnp.zeros_like(x2))], axis=0)
            y_ref[:, lo:lo + 2 * SSD_HEAD_DIM] = (
                _dot(w_pair, rhs) + y_off[:, j * 2 * SSD_HEAD_DIM:(j + 1) * 2 * SSD_HEAD_DIM])
        new_states = _dot_tn(bg, xdec[:, g * gw:(g + 1) * gw])
        etot = ecs_x[n - 1:n, g * gw:(g + 1) * gw] if fwd else ecs_x[0:1, g * gw:(g + 1) * gw]
        st_ref[g] = st * etot + new_states


def _ssd_kernel(*refs):
    n_in = 9
    expand_ref = refs[2 * n_in]
    st_ref = refs[-1]

    @pl.when(pl.program_id(1) == 0)
    def _():
        st_ref[...] = jnp.zeros_like(st_ref)

    for d in range(2):
        _ssd_chunk(d == 0, *refs[d * n_in:(d + 1) * n_in], expand_ref, refs[2 * n_in + 1 + d], st_ref.at[d])


def _ssd(xbc, dt, dt_t, bias, bias_t, alog, alog_t, *, layer, n_batch, seq_len, ctx_len):
    rows = xbc.shape[0]
    ncc = ctx_len // SSD_CHUNK
    ncl = seq_len // SSD_CHUNK
    lat_blk = seq_len // SSD_CHUNK
    ctx_blk0 = n_batch * seq_len // SSD_CHUNK
    bcol = SSD_DIM // (SSD_GROUPS * SSD_STATE)

    def rb(d):
        def block(b, c):
            cc = c if d == 0 else ncc - 1 - c
            lc = c - ncc if d == 0 else ncl - 1 - (c - ncc)
            return jnp.where(c < ncc, ctx_blk0 + b * ncc + cc, b * lat_blk + lc)
        return block

    def operand_specs(d):
        blk = rb(d)
        return [
            pl.BlockSpec((SSD_CHUNK, SSD_DIM), lambda b, c: (blk(b, c), 0)),
            pl.BlockSpec((SSD_CHUNK, SSD_GROUPS * SSD_STATE), lambda b, c: (blk(b, c), bcol)),
            pl.BlockSpec((SSD_CHUNK, SSD_GROUPS * SSD_STATE), lambda b, c: (blk(b, c), bcol + 1)),
            pl.BlockSpec((SSD_CHUNK, DT_W), lambda b, c: (blk(b, c), 0)),
            pl.BlockSpec((SSD_HEADS, SSD_CHUNK), lambda b, c: (d, blk(b, c))),
            pl.BlockSpec((None, None, 1, DT_W), lambda b, c: (layer, d, 0, 0)),
            pl.BlockSpec((None, None, SSD_HEADS, 1), lambda b, c: (layer, d, 0, 0)),
            pl.BlockSpec((None, None, 1, DT_W), lambda b, c: (layer, d, 0, 0)),
            pl.BlockSpec((None, None, SSD_HEADS, 1), lambda b, c: (layer, d, 0, 0)),
        ]

    operands = (xbc, xbc, xbc, dt, dt_t, bias, bias_t, alog, alog_t)
    y_shape = jax.ShapeDtypeStruct((rows, SSD_DIM), F32)
    return pl.pallas_call(
        _ssd_kernel,
        grid=(n_batch, ncc + ncl),
        in_specs=operand_specs(0) + operand_specs(1) + [
            pl.BlockSpec((2 * DT_W, 3 * SSD_DIM), lambda b, c: (0, 0))],
        out_specs=[pl.BlockSpec((SSD_CHUNK, SSD_DIM), lambda b, c: (rb(0)(b, c), 0)),
                   pl.BlockSpec((SSD_CHUNK, SSD_DIM), lambda b, c: (rb(1)(b, c), 0))],
        out_shape=[y_shape, y_shape],
        scratch_shapes=[pltpu.VMEM((2, SSD_GROUPS, SSD_STATE, HEADS_PER_GROUP * SSD_HEAD_DIM), F32)],
        compiler_params=_cparams(("parallel", "arbitrary")),
        name="ssd",
    )(*operands, *operands, _ssd_expand_table())


def _fourier_fold(u_refs, chan_ref, p_ref, ah_ref, n):
    gd = FOURIER_GROUP_DIM
    half = n // 2
    bs = min(256, half)
    nb = half // bs
    r_i = lax.broadcasted_iota(jnp.int32, (bs, bs), 0)
    c_i = lax.broadcasted_iota(jnp.int32, (bs, bs), 1)
    flip = (r_i + c_i == bs).astype(BF16)
    row0 = lax.broadcasted_iota(jnp.int32, (bs, gd), 0) == 0
    cos_c = chan_ref[:, :gd]
    sin_c = chan_ref[:, gd:]
    for g, u_ref in enumerate(u_refs):
        cols = slice(g * gd, (g + 1) * gd)
        for i in range(nb):
            lo = u_ref[bs * i:bs * (i + 1), :].astype(F32)
            rev = _dot(flip, u_ref[bs * (2 * nb - 1 - i):bs * (2 * nb - i), :])
            if i > 0:
                first = u_ref[bs * (2 * nb - i):bs * (2 * nb - i) + 16, :][0:1, :].astype(F32)
                rev = jnp.where(row0, first, rev)
            p_ref[bs * i:bs * (i + 1), cols] = _dot((lo + rev).astype(BF16), cos_c).astype(BF16)
            p_ref[half + bs * i:half + bs * (i + 1), cols] = _dot((lo - rev).astype(BF16), sin_c).astype(BF16)
        ah_ref[:, cols] = _dot(u_ref[half:half + 16, :], cos_c)


def _fourier_rows(m_ref, p_ref, ah_ref, o_ref, n):
    rows = m_ref.shape[0]
    parity = lax.broadcasted_iota(jnp.int32, (rows, 1), 0) & 1
    sign = (1 - 2 * parity).astype(F32)
    nyquist = ah_ref[0:1, :] * (n ** -0.5)
    o_ref[...] = (_dot(m_ref[...], p_ref[0:n, :]) + sign * nyquist).astype(BF16)


def _fourier_kernel(ul0, ul1, ul2, ul3, uc0, uc1, uc2, uc3, chan_ref, ml_ref, mc_ref, o_ref, p_ref, ah_ref, *,
                    nm, seq_len, ctx_len):
    mi = pl.program_id(1)

    @pl.when(mi == 0)
    def _():
        _fourier_fold((ul0, ul1, ul2, ul3), chan_ref, p_ref, ah_ref, seq_len)

    @pl.when(mi < nm)
    def _():
        _fourier_rows(ml_ref, p_ref, ah_ref, o_ref, seq_len)

    @pl.when(mi == nm)
    def _():
        _fourier_fold((uc0, uc1, uc2, uc3), chan_ref, p_ref, ah_ref, ctx_len)
        _fourier_rows(mc_ref, p_ref, ah_ref, o_ref, ctx_len)


def _fourier(proj, chan_mat, seq_mat_l, seq_mat_c, *, n_batch, seq_len, ctx_len):
    rows = proj.shape[0]
    tm = TM_FOURIER
    nm = seq_len // tm
    gd = FOURIER_GROUP_DIM
    col0 = COL_U // gd
    ctx_blk0 = n_batch * seq_len // ctx_len

    def u_specs(block_rows, blk0):
        return [pl.BlockSpec((block_rows, gd), functools.partial(lambda b, mi, g: (blk0 + b, col0 + g), g=g))
                for g in range(FOURIER_GROUPS)]

    def out_map(b, mi):
        return (jnp.where(mi < nm, b * nm + mi, n_batch * nm + b), 0)

    return pl.pallas_call(
        functools.partial(_fourier_kernel, nm=nm, seq_len=seq_len, ctx_len=ctx_len),
        grid=(n_batch, nm + 1),
        in_specs=u_specs(seq_len, 0) + u_specs(ctx_len, ctx_blk0) + [
            pl.BlockSpec((gd, 2 * gd), lambda b, mi: (0, 0)),
            pl.BlockSpec((tm, seq_len), lambda b, mi: (jnp.minimum(mi, nm - 1), 0)),
            pl.BlockSpec((ctx_len, ctx_len), lambda b, mi: (0, 0)),
        ],
        out_specs=pl.BlockSpec((tm, FOURIER_DIM), out_map),
        out_shape=jax.ShapeDtypeStruct((rows, FOURIER_DIM), BF16),
        scratch_shapes=[pltpu.VMEM((seq_len, FOURIER_DIM), BF16), pltpu.VMEM((16, FOURIER_DIM), F32)],
        compiler_params=_cparams(("parallel", "arbitrary")),
        name="fourier",
    )(*([proj] * (2 * FOURIER_GROUPS)), chan_mat, seq_mat_l, seq_mat_c)


def _dft_cos_sin(n, n_cols):
    j = jnp.arange(n, dtype=jnp.int32)
    jk = (j[:, None] * j[None, :n_cols]) % n
    ang = jk.astype(F32) * (2.0 * math.pi / n)
    return jnp.cos(ang), jnp.sin(ang)


def _dft_tables(n):
    cos_m, sin_m = _dft_cos_sin(n, n // 2)
    return (jnp.concatenate([cos_m, -sin_m], axis=1) * (n ** -0.5)).astype(BF16)


def _chan_table():
    cos_m, sin_m = _dft_cos_sin(FOURIER_GROUP_DIM, FOURIER_GROUP_DIM)
    return (jnp.concatenate([cos_m, sin_m], axis=1) * (FOURIER_GROUP_DIM ** -0.5)).astype(BF16)


def _merge_kernel(x_ref, mod_ref, yf_ref, yb_ref, xh_ref, z_ref, four_ref, attn_ref, g_ref, wb_ref, wo_ref,
                  dsk_ref, ng_ref, lg_ref, lb_ref, o_ref):
    y = yf_ref[...] + yb_ref[...] + dsk_ref[...] * xh_ref[...].astype(F32)
    y = y * _silu(z_ref[...].astype(F32))
    gw = SSD_DIM // SSD_GROUPS
    heads = []
    for g in range(SSD_GROUPS):
        yg = y[:, g * gw:(g + 1) * gw]
        yg = yg * lax.rsqrt(jnp.mean(yg * yg, axis=-1, keepdims=True) + RMS_EPS)
        heads.append((yg * ng_ref[:, g * gw:(g + 1) * gw]).astype(BF16))
    branches = (jnp.concatenate(heads, axis=1), four_ref[...], attn_ref[...])

    acc = None
    for j in range(D_MODEL // TN_MERGE):
        lo = j * TN_MERGE
        m = None
        for k, yk in enumerate(branches):
            gate = jax.nn.sigmoid(g_ref[:, k * D_MODEL + lo:k * D_MODEL + lo + TN_MERGE].astype(F32))
            term = gate * _dot(yk, wb_ref[k, :, lo:lo + TN_MERGE])
            m = term if m is None else m + term
        part = _dot(m.astype(BF16), wo_ref[lo:lo + TN_MERGE, :])
        acc = part if acc is None else acc + part

    v = ALPHA * x_ref[...] + mod_ref[5:6, :] * acc
    o_ref[...] = _layer_norm(v, lg_ref[...], lb_ref[...])


def _merge(xs, rows, mods, y_ssd, xbc, proj, four, attn, w_branch, w_out, dsk, norm_g, ln_g, ln_b, *,
           layer, seq_len, n_batch):
    tiles_per_batch = seq_len // TM_MERGE
    resident = pl.Buffered(1)
    row_spec = pl.BlockSpec((TM_MERGE, BRANCH_DIM), lambda i: (i, 0))
    vec_spec = pl.BlockSpec((None, 1, BRANCH_DIM), lambda i: (layer, 0, 0))
    ln_spec = pl.BlockSpec((None, None, 1, D_MODEL), lambda i: (layer, 1, 0, 0))
    return pl.pallas_call(
        _merge_kernel,
        grid=(rows // TM_MERGE,),
        in_specs=[
            pl.BlockSpec((TM_MERGE, D_MODEL), lambda i: (i, 0)),
            pl.BlockSpec((None, None, N_MOD, D_MODEL),
                         lambda i: (layer, jnp.minimum(i // tiles_per_batch, n_batch), 0, 0)),
            row_spec,
            row_spec,
            row_spec,
            pl.BlockSpec((TM_MERGE, BRANCH_DIM), lambda i: (i, COL_Z // BRANCH_DIM)),
            row_spec,
            row_spec,
            pl.BlockSpec((TM_MERGE, N_BRANCH * D_MODEL), lambda i: (i, COL_G // (N_BRANCH * D_MODEL))),
            pl.BlockSpec((None, N_BRANCH, BRANCH_DIM, D_MODEL), lambda i: (layer, 0, 0, 0), pipeline_mode=resident),
            pl.BlockSpec((None, D_MODEL, D_MODEL), lambda i: (layer, 0, 0), pipeline_mode=resident),
            vec_spec, vec_spec,
            ln_spec, ln_spec,
        ],
        out_specs=pl.BlockSpec((TM_MERGE, D_MODEL), lambda i: (i, 0)),
        out_shape=jax.ShapeDtypeStruct((rows, D_MODEL), F32),
        compiler_params=_cparams(("parallel",)),
        name="merge",
    )(xs, mods, y_ssd[0], y_ssd[1], xbc, proj, four, attn, proj, w_branch, w_out, dsk, norm_g, ln_g, ln_b)


def kernel(x, c, ctx, c_ctx, w_ada, b_ada, ln_g, ln_b, ffn_wgu, ffn_wd, w_in, conv_w, conv_b, dt_bias, a_log,
           d_skip, ssd_norm_g, q_norm_g, k_norm_g, w_branch, w_out):
    n_batch, seq_len, _ = x.shape
    ctx_len = ctx.shape[1]
    lat_rows = n_batch * seq_len
    rows = lat_rows + n_batch * ctx_len
    assert ctx_len == TQ and seq_len % TM == 0 and (n_batch * ctx_len) % TM == 0 and n_batch < MOD_ROWS
    tiles_per_batch = seq_len // TM
    tile_kw = dict(tiles_per_batch=tiles_per_batch, n_batch=n_batch)

    wgu_bf = ffn_wgu.astype(BF16)
    wd_bf = ffn_wd.astype(BF16)
    wb_bf = w_branch.astype(BF16)
    wo_bf = w_out.astype(BF16)
    w_z, w_xbc, w_dtf, w_dtb, w_u, w_q, w_k, w_v, w_g = jnp.split(w_in, IN_SPLITS, axis=-1)
    w_main = jnp.concatenate([w_g, w_z, w_u, w_q, w_xbc, w_k, w_v], axis=-1).astype(BF16)
    w_dt = jnp.concatenate([w_dtf, w_dtb, jnp.zeros((DEPTH, D_MODEL, DT_W - 2 * SSD_HEADS), F32)],
                           axis=-1).astype(BF16)
    head_pad = ((0, 0), (0, 0), (0, DT_W - SSD_HEADS))
    bias_row = jnp.pad(dt_bias, head_pad).reshape(DEPTH, 2, 1, DT_W)
    alog_row = jnp.pad(a_log, head_pad).reshape(DEPTH, 2, 1, DT_W)
    bias_col = dt_bias.reshape(DEPTH, 2, SSD_HEADS, 1)
    alog_col = a_log.reshape(DEPTH, 2, SSD_HEADS, 1)
    dsk_all = jnp.repeat(d_skip, SSD_HEAD_DIM, axis=-1).reshape(DEPTH, 1, SSD_DIM)
    norm_g_all = ssd_norm_g.reshape(DEPTH, 1, SSD_DIM)
    qk_gains = jnp.stack([q_norm_g, k_norm_g], axis=1)
    ln_g4 = ln_g.reshape(DEPTH, 3, 1, D_MODEL)
    ln_b4 = ln_b.reshape(DEPTH, 3, 1, D_MODEL)
    cos_t, sin_t = _rope_tables(seq_len)
    chan_mat = _chan_table()
    seq_mat_l = _dft_tables(seq_len)
    seq_mat_c = _dft_tables(ctx_len)

    cvec = jnp.concatenate([c, c_ctx[None], jnp.zeros((MOD_ROWS - n_batch - 1, D_MODEL), F32)], axis=0)
    mods = _adaln_all(cvec, w_ada, b_ada)

    xs = jnp.concatenate([x.reshape(lat_rows, D_MODEL), ctx.reshape(n_batch * ctx_len, D_MODEL)], axis=0)
    for layer in range(DEPTH):
        last = layer == DEPTH - 1
        xs = _ffn(xs, rows, mods, wgu_bf, wd_bf, ln_g4, ln_b4, layer=layer, which=0, sub=0, **tile_kw)

        proj, dt = _inproj(xs, mods, w_main, w_dt, layer=layer, **tile_kw)
        xbc = _conv(proj, conv_w, conv_b, layer=layer, lat_rows=lat_rows, seq_len=seq_len)
        y_ssd = _ssd(xbc, dt, dt.T, bias_row, bias_col, alog_row, alog_col,
                     layer=layer, n_batch=n_batch, seq_len=seq_len, ctx_len=ctx_len)
        qk = _qk_prep(proj, qk_gains[layer], cos_t, sin_t, lat_rows=lat_rows, seq_len=seq_len)
        attn = _attention(qk, proj, n_batch=n_batch, seq_len=seq_len, ctx_len=ctx_len)
        four = _fourier(proj, chan_mat, seq_mat_l, seq_mat_c, n_batch=n_batch, seq_len=seq_len, ctx_len=ctx_len)
        out_rows = lat_rows if last else rows
        xs = _merge(xs, out_rows, mods, y_ssd, xbc, proj, four, attn, wb_bf, wo_bf, dsk_all, norm_g_all,
                    ln_g4, ln_b4, layer=layer, seq_len=seq_len, n_batch=n_batch)
        xs = _ffn(xs, out_rows, mods, wgu_bf, wd_bf, ln_g4, ln_b4, layer=layer, which=1, sub=2, **tile_kw)
    return xs.reshape(n_batch, seq_len, D_MODEL)
```
